```python
import math
import jax
import jax.numpy as jnp
from jax import lax
import numpy as np

D_MODEL = 1024
BATCH = 16
SEQ = 2048
DEPTH = 2

GRID_W = 64
CTX_LEN = 256
EPS = 1e-6

N_BRANCH = 4
BRANCH_W = D_MODEL // 4

MLA_HEADS = 4
MLA_NOPE = 64
MLA_ROPE = 32
MLA_V = 64
MLA_QK = MLA_NOPE + MLA_ROPE
Q_LORA = 256
KV_LORA = 128
ROPE_THETA = 10000.0
Q_BLOCK = 128

S5_GROUP = 16
S5_GROUPS = BRANCH_W // S5_GROUP
S5_STATE = 64

LRU_BLOCKS = 4
LRU_BLOCK_W = BRANCH_W // LRU_BLOCKS
LRU_CONV = 4
LRU_C = 8.0

POOL_WINDOWS = (2, 4, 8, 16)
POOL_GROUP_W = BRANCH_W // len(POOL_WINDOWS)

OFF_KROPE = KV_LORA
OFF_S5 = OFF_KROPE + MLA_ROPE
OFF_LRU = OFF_S5 + BRANCH_W
OFF_CQ = OFF_LRU + BRANCH_W
OFF_POOL = OFF_CQ + Q_LORA
OFF_GATE = OFF_POOL + BRANCH_W
OFF_MERGE = OFF_GATE + N_BRANCH * BRANCH_W
IN_W = OFF_MERGE + N_BRANCH * D_MODEL
IN_SPLITS = (OFF_KROPE, OFF_S5, OFF_LRU, OFF_CQ, OFF_POOL, OFF_GATE, OFF_MERGE)
MEM_W = OFF_CQ
MEM_SPLITS = (OFF_KROPE, OFF_S5, OFF_LRU)

kernel_name = 'hybrid_mla_s5_rglru_pool_diffusion_block'


def _rmsnorm(x, g):
    x32 = x.astype(jnp.float32)
    y = x32 * lax.rsqrt(jnp.mean(x32 * x32, axis=-1, keepdims=True) + EPS)
    return (y * g.astype(jnp.float32)).astype(x.dtype)


def _rope_tables(L):
    rows_n = L // GRID_W
    row = jnp.repeat(jnp.arange(rows_n, dtype=jnp.int32), GRID_W).astype(jnp.float32)
    col = jnp.tile(jnp.arange(GRID_W, dtype=jnp.int32), rows_n).astype(jnp.float32)
    nf = MLA_ROPE // 4
    inv = ROPE_THETA ** (-jnp.arange(nf, dtype=jnp.float32) / nf)
    ang_r = row[:, None] * inv
    ang_c = col[:, None] * inv
    return (jnp.cos(ang_r), jnp.sin(ang_r), jnp.cos(ang_c), jnp.sin(ang_c))


def _rot_half(v, cos, sin):
    nf = v.shape[-1] // 2
    v1, v2 = v[..., :nf], v[..., nf:]
    return jnp.concatenate([v1 * cos - v2 * sin, v1 * sin + v2 * cos], axis=-1)


def _axial_rope(v, tables):
    cr, sr, cc, sc = [t[None, :, None, :].astype(v.dtype) for t in tables]
    half = MLA_ROPE // 2
    return jnp.concatenate([_rot_half(v[..., :half], cr, sr), _rot_half(v[..., half:], cc, sc)], axis=-1)


def _mla_kv(ckv, krope, kv_norm_g, w_ukv, k_gain, tables):
    B, L, _ = ckv.shape
    kv = (_rmsnorm(ckv, kv_norm_g) @ w_ukv).reshape(B, L, MLA_HEADS, MLA_NOPE + MLA_V)
    k_nope, v = kv[..., :MLA_NOPE], kv[..., MLA_NOPE:]
    k_r = jnp.broadcast_to(krope[:, :, None, :], (B, L, MLA_HEADS, MLA_ROPE))
    k = _rmsnorm(jnp.concatenate([k_nope, k_r], axis=-1), k_gain)
    if tables is not None:
        k = jnp.concatenate([k[..., :MLA_NOPE], _axial_rope(k[..., MLA_NOPE:], tables)], axis=-1)
    return k, v


def _mla_q(cq, q_norm_g, w_uq, q_gain, tables):
    B, L, _ = cq.shape
    q = (_rmsnorm(cq, q_norm_g) @ w_uq).reshape(B, L, MLA_HEADS, MLA_QK)
    q = _rmsnorm(q, q_gain)
    if tables is not None:
        q = jnp.concatenate([q[..., :MLA_NOPE], _axial_rope(q[..., MLA_NOPE:], tables)], axis=-1)
    return q


def _attend(q, k, v):
    s = jnp.einsum('bqhd,bkhd->bhqk', q, k).astype(jnp.float32) * (MLA_QK ** -0.5)
    p = jax.nn.softmax(s, axis=-1).astype(v.dtype)
    return jnp.einsum('bhqk,bkhd->bqhd', p, v)


def _attend_blocked(q, k, v):
    B, L, H, Dq = q.shape
    nb = L // Q_BLOCK
    qb = q.reshape(B, nb, Q_BLOCK, H, Dq).transpose(1, 0, 2, 3, 4)
    o = lax.map(lambda qi: _attend(qi, k, v), qb)
    return o.transpose(1, 0, 2, 3, 4).reshape(B, L, H * v.shape[-1])


def _complex_scan(a_re, a_im, b_re, b_im, h0_re, h0_im, reverse):
    def comb(e1, e2):
        a1r, a1i, b1r, b1i = e1
        a2r, a2i, b2r, b2i = e2
        return (a2r * a1r - a2i * a1i, a2r * a1i + a2i * a1r,
                a2r * b1r - a2i * b1i + b2r, a2r * b1i + a2i * b1r + b2i)
    a_re = jnp.broadcast_to(a_re, b_re.shape)
    a_im = jnp.broadcast_to(a_im, b_re.shape)
    ar, ai, br, bi = lax.associative_scan(comb, (a_re, a_im, b_re, b_im), reverse=reverse, axis=1)
    h0r, h0i = h0_re[:, None], h0_im[:, None]
    return ar * h0r - ai * h0i + br, ar * h0i + ai * h0r + bi


def _real_scan(a, b, h0, reverse):
    def comb(e1, e2):
        a1, b1 = e1
        a2, b2 = e2
        return a1 * a2, a2 * b1 + b2
    ac, bc = lax.associative_scan(comb, (a, b), reverse=reverse, axis=1)
    return ac * h0[:, None] + bc


def _s5_discretize(a_re, a_im, log_dt, b_re, b_im):
    f32 = jnp.float32
    a_re, a_im, b_re, b_im = a_re.astype(f32), a_im.astype(f32), b_re.astype(f32), b_im.astype(f32)
    dt = jnp.exp(log_dt.astype(f32))[:, None]
    mag = jnp.exp(a_re * dt)
    ab_re = mag * jnp.cos(a_im * dt)
    ab_im = mag * jnp.sin(a_im * dt)
    den = a_re * a_re + a_im * a_im
    f_re = ((ab_re - 1.0) * a_re + ab_im * a_im) / den
    f_im = (ab_im * a_re - (ab_re - 1.0) * a_im) / den
    bb_re = f_re[..., None] * b_re - f_im[..., None] * b_im
    bb_im = f_re[..., None] * b_im + f_im[..., None] * b_re
    return ab_re, ab_im, bb_re, bb_im


def _s5_states(ug, h0_re, h0_im, disc, reverse):
    ab_re, ab_im, bb_re, bb_im = disc
    bu_re = jnp.einsum('blgi,gpi->blgp', ug, bb_re)
    bu_im = jnp.einsum('blgi,gpi->blgp', ug, bb_im)
    return _complex_scan(ab_re, ab_im, bu_re, bu_im, h0_re, h0_im, reverse)


def _s5_readout(h_re, h_im, c_re, c_im):
    return jnp.einsum('gip,blgp->blgi', c_re, h_re) - jnp.einsum('gip,blgp->blgi', c_im, h_im)


def _s5_glu(y, u, d, w_glu):
    f32 = jnp.float32
    g = jax.nn.gelu(y + d.astype(f32) * u.astype(f32))
    return (g * jax.nn.sigmoid(g @ w_glu.astype(f32))).astype(u.dtype)


def _s5_branch(u, uc, a_re, a_im, log_dt, b_re, b_im, c_re, c_im, d, w_glu, with_ctx):
    f32 = jnp.float32
    B, L, W = u.shape
    Lc = uc.shape[1]
    ug = u.astype(f32).reshape(B, L, S5_GROUPS, S5_GROUP)
    ucg = uc.astype(f32).reshape(B, Lc, S5_GROUPS, S5_GROUP)
    zero = jnp.zeros((B, S5_GROUPS, S5_STATE), f32)
    y = 0.0
    yc = 0.0
    for dr in range(2):
        rev = dr == 1
        disc = _s5_discretize(a_re[dr], a_im[dr], log_dt[dr], b_re[dr], b_im[dr])
        hc_re, hc_im = _s5_states(ucg, zero, zero, disc, rev)
        last = 0 if rev else Lc - 1
        hl_re, hl_im = _s5_states(ug, hc_re[:, last], hc_im[:, last], disc, rev)
        y = y + _s5_readout(hl_re, hl_im, c_re[dr], c_im[dr])
        if with_ctx:
            yc = yc + _s5_readout(hc_re, hc_im, c_re[dr], c_im[dr])
    out = _s5_glu(y.reshape(B, L, W), u, d, w_glu)
    out_c = _s5_glu(yc.reshape(B, Lc, W), uc, d, w_glu) if with_ctx else None
    return out, out_c


def _short_conv(x, w, b):
    L = x.shape[1]
    left = LRU_CONV // 2
    xp = jnp.pad(x, ((0, 0), (left, LRU_CONV - 1 - left), (0, 0)))
    out = b
    for k in range(LRU_CONV):
        out = out + xp[:, k:k + L] * w[k]
    return out


def _block_diag(x, w, b):
    B, L, W = x.shape
    y = jnp.einsum('blnj,njk->blnk', x.reshape(B, L, LRU_BLOCKS, LRU_BLOCK_W), w)
    return y.reshape(B, L, W) + b


def _rglru_states(x, h0, lam, w_a, b_a, w_x, b_x, reverse):
    f32 = jnp.float32
    r = jax.nn.sigmoid(_block_diag(x, w_a, b_a).astype(f32))
    i = jax.nn.sigmoid(_block_diag(x, w_x, b_x).astype(f32))
    log_a = -LRU_C * r * jax.nn.softplus(-lam.astype(f32))
    a = jnp.exp(log_a)
    b = jnp.sqrt(-jnp.expm1(2.0 * log_a)) * (i * x.astype(f32))
    return _real_scan(a, b, h0, reverse)


def _lru_branch(x, xc, conv_w, conv_b, lam, w_a, b_a, w_x, b_x, with_ctx):
    xl = _short_conv(x, conv_w, conv_b)
    xcc = _short_conv(xc, conv_w, conv_b)
    B, Lc, W = xc.shape
    zero = jnp.zeros((B, W), jnp.float32)
    y = 0.0
    yc = 0.0
    for dr in range(2):
        rev = dr == 1
        hc = _rglru_states(xcc, zero, lam[dr], w_a[dr], b_a[dr], w_x[dr], b_x[dr], rev)
        last = 0 if rev else Lc - 1
        y = y + _rglru_states(xl, hc[:, last], lam[dr], w_a[dr], b_a[dr], w_x[dr], b_x[dr], rev)
        if with_ctx:
            yc = yc + hc
    return y.astype(x.dtype), (yc.astype(xc.dtype) if with_ctx else None)


def _pool_mix(x, w, b, scale):
    f32 = jnp.float32
    B, L, W = x.shape
    x32 = x.astype(f32)
    cs = jnp.concatenate([jnp.zeros((B, 1, W), f32), jnp.cumsum(x32, axis=1)], axis=1)
    t = jnp.arange(L, dtype=jnp.int32)
    parts = []
    for gi, win in enumerate(POOL_WINDOWS):
        sl = slice(gi * POOL_GROUP_W, (gi + 1) * POOL_GROUP_W)
        lo = jnp.clip(t - win // 2, 0, L)
        hi = jnp.clip(t + win // 2, 0, L)
        csg = cs[..., sl]
        s = jnp.take(csg, hi, axis=1) - jnp.take(csg, lo, axis=1)
        cnt = (hi - lo).astype(f32)[None, :, None]
        parts.append(s / cnt - x32[..., sl])
    p = jnp.stack(parts, axis=2)
    y = jnp.einsum('blgi,gio->blgo', p, w).reshape(B, L, W) + b
    return (y * scale).astype(x.dtype)


def _merge(branches, gate_paths, merge_logits, w_branch, w_out):
    gp = jnp.split(gate_paths, N_BRANCH, axis=-1)
    ml = jnp.split(merge_logits, N_BRANCH, axis=-1)
    y = 0.0
    for n in range(N_BRANCH):
        y = y + jax.nn.sigmoid(ml[n]) * ((branches[n] * jax.nn.silu(gp[n])) @ w_branch[n])
    return y @ w_out


def setup_inputs(seed: int = 0) -> dict:
    key = jax.random.key(seed)
    ks = iter(jax.random.split(key, 48))
    f32 = jnp.float32

    def nrm(shape, s):
        return jax.random.normal(next(ks), shape, f32) * s

    W, G, P = BRANCH_W, S5_GROUPS, S5_STATE
    x = nrm((BATCH, SEQ, D_MODEL), 1.0)
    c = nrm((BATCH, D_MODEL), 1.0)
    ctx = nrm((BATCH, CTX_LEN, D_MODEL), 1.0)
    c_ctx = nrm((D_MODEL,), 1.0)
    w_ada = nrm((DEPTH, D_MODEL, 3 * D_MODEL), 0.5 * D_MODEL ** -0.5)
    b_ada = nrm((DEPTH, 3 * D_MODEL), 0.01)
    norm_g = 1.0 + nrm((DEPTH, D_MODEL), 0.05)
    w_in = nrm((DEPTH, D_MODEL, IN_W), D_MODEL ** -0.5)
    mla_q_norm = 1.0 + nrm((DEPTH, Q_LORA), 0.05)
    mla_kv_norm = 1.0 + nrm((DEPTH, KV_LORA), 0.05)
    mla_w_uq = nrm((DEPTH, Q_LORA, MLA_HEADS * MLA_QK), Q_LORA ** -0.5)
    mla_w_ukv = nrm((DEPTH, KV_LORA, MLA_HEADS * (MLA_NOPE + MLA_V)), KV_LORA ** -0.5)
    mla_q_gain = 1.0 + nrm((DEPTH, MLA_QK), 0.05)
    mla_k_gain = 1.0 + nrm((DEPTH, MLA_QK), 0.05)
    n_idx = jnp.arange(P, dtype=f32)
    s5_a_re = -0.5 + nrm((DEPTH, 2, G, P), 0.01)
    s5_a_im = math.pi * n_idx + nrm((DEPTH, 2, G, P), 0.01)
    s5_log_dt = jax.random.uniform(next(ks), (DEPTH, 2, G), f32, math.log(1e-3), math.log(1e-1))
    s5_b_re = nrm((DEPTH, 2, G, P, S5_GROUP), (2.0 * S5_GROUP) ** -0.5)
    s5_b_im = nrm((DEPTH, 2, G, P, S5_GROUP), (2.0 * S5_GROUP) ** -0.5)
    s5_c_re = nrm((DEPTH, 2, G, S5_GROUP, P), P ** -0.5)
    s5_c_im = nrm((DEPTH, 2, G, S5_GROUP, P), P ** -0.5)
    s5_d = nrm((DEPTH, W), 1.0)
    s5_w_glu = nrm((DEPTH, W, W), W ** -0.5)
    lru_conv_w = nrm((DEPTH, LRU_CONV, W), 0.5)
    lru_conv_b = nrm((DEPTH, W), 0.01)
    a0 = jax.random.uniform(next(ks), (DEPTH, 2, W), f32, 0.9, 0.999)
    s0 = a0 ** (1.0 / LRU_C)
    lru_lambda = jnp.log(s0) - jnp.log1p(-s0)
    lru_w_a = nrm((DEPTH, 2, LRU_BLOCKS, LRU_BLOCK_W, LRU_BLOCK_W), LRU_BLOCK_W ** -0.5)
    lru_b_a = nrm((DEPTH, 2, W), 0.01)
    lru_w_x = nrm((DEPTH, 2, LRU_BLOCKS, LRU_BLOCK_W, LRU_BLOCK_W), LRU_BLOCK_W ** -0.5)
    lru_b_x = nrm((DEPTH, 2, W), 0.01)
    pool_w = nrm((DEPTH, len(POOL_WINDOWS), POOL_GROUP_W, POOL_GROUP_W), POOL_GROUP_W ** -0.5)
    pool_b = nrm((DEPTH, W), 0.01)
    pool_scale = 1.0 + nrm((DEPTH, W), 0.05)
    w_branch = nrm((DEPTH, N_BRANCH, W, D_MODEL), W ** -0.5)
    w_out = nrm((DEPTH, D_MODEL, D_MODEL), D_MODEL ** -0.5)
    return {'x': x, 'c': c, 'ctx': ctx, 'c_ctx': c_ctx,
            'w_ada': w_ada, 'b_ada': b_ada, 'norm_g': norm_g, 'w_in': w_in,
            'mla_q_norm': mla_q_norm, 'mla_kv_norm': mla_kv_norm, 'mla_w_uq': mla_w_uq,
            'mla_w_ukv': mla_w_ukv, 'mla_q_gain': mla_q_gain, 'mla_k_gain': mla_k_gain,
            's5_a_re': s5_a_re, 's5_a_im': s5_a_im, 's5_log_dt': s5_log_dt,
            's5_b_re': s5_b_re, 's5_b_im': s5_b_im, 's5_c_re': s5_c_re, 's5_c_im': s5_c_im,
            's5_d': s5_d, 's5_w_glu': s5_w_glu,
            'lru_conv_w': lru_conv_w, 'lru_conv_b': lru_conv_b, 'lru_lambda': lru_lambda,
            'lru_w_a': lru_w_a, 'lru_b_a': lru_b_a, 'lru_w_x': lru_w_x, 'lru_b_x': lru_b_x,
            'pool_w': pool_w, 'pool_b': pool_b, 'pool_scale': pool_scale,
            'w_branch': w_branch, 'w_out': w_out}


def reference(x, c, ctx, c_ctx, w_ada, b_ada, norm_g, w_in,
              mla_q_norm, mla_kv_norm, mla_w_uq, mla_w_ukv, mla_q_gain, mla_k_gain,
              s5_a_re, s5_a_im, s5_log_dt, s5_b_re, s5_b_im, s5_c_re, s5_c_im, s5_d, s5_w_glu,
              lru_conv_w, lru_conv_b, lru_lambda, lru_w_a, lru_b_a, lru_w_x, lru_b_x,
              pool_w, pool_b, pool_scale, w_branch, w_out):
    B, L, _ = x.shape
    Lc = ctx.shape[1]
    c_act = jax.nn.silu(c)
    cctx_act = jax.nn.silu(c_ctx)
    tables = _rope_tables(L)
    xc = ctx
    for l in range(DEPTH):
        with_ctx = l < DEPTH - 1
        shift, scale, gate = jnp.split(c_act @ w_ada[l] + b_ada[l], 3, axis=-1)
        shift_c, scale_c, gate_c = jnp.split(cctx_act @ w_ada[l] + b_ada[l], 3, axis=-1)
        h = _rmsnorm(x, norm_g[l]) * (1.0 + scale[:, None]) + shift[:, None]
        hc = _rmsnorm(xc, norm_g[l]) * (1.0 + scale_c) + shift_c
        ckv, krope, u_s5, x_lru, cq, x_pool, gpath, mlogit = jnp.split(h @ w_in[l], IN_SPLITS, axis=-1)
        if with_ctx:
            ckv_c, krope_c, u_s5_c, x_lru_c, cq_c, x_pool_c, gpath_c, mlogit_c = jnp.split(
                hc @ w_in[l], IN_SPLITS, axis=-1)
        else:
            ckv_c, krope_c, u_s5_c, x_lru_c = jnp.split(hc @ w_in[l, :, :MEM_W], MEM_SPLITS, axis=-1)

        k_l, v_l = _mla_kv(ckv, krope, mla_kv_norm[l], mla_w_ukv[l], mla_k_gain[l], tables)
        k_c, v_c = _mla_kv(ckv_c, krope_c, mla_kv_norm[l], mla_w_ukv[l], mla_k_gain[l], None)
        q_l = _mla_q(cq, mla_q_norm[l], mla_w_uq[l], mla_q_gain[l], tables)
        o_mla = _attend_blocked(q_l, jnp.concatenate([k_c, k_l], axis=1), jnp.concatenate([v_c, v_l], axis=1))

        o_s5, o_s5_c = _s5_branch(u_s5, u_s5_c, s5_a_re[l], s5_a_im[l], s5_log_dt[l], s5_b_re[l], s5_b_im[l],
                                  s5_c_re[l], s5_c_im[l], s5_d[l], s5_w_glu[l], with_ctx)
        o_lru, o_lru_c = _lru_branch(x_lru, x_lru_c, lru_conv_w[l], lru_conv_b[l], lru_lambda[l],
                                     lru_w_a[l], lru_b_a[l], lru_w_x[l], lru_b_x[l], with_ctx)
        o_pool = _pool_mix(x_pool, pool_w[l], pool_b[l], pool_scale[l])

        y = _merge((o_mla, o_s5, o_lru, o_pool), gpath, mlogit, w_branch[l], w_out[l])
        if with_ctx:
            q_c = _mla_q(cq_c, mla_q_norm[l], mla_w_uq[l], mla_q_gain[l], None)
            o_mla_c = _attend(q_c, k_c, v_c).reshape(B, Lc, MLA_HEADS * MLA_V)
            o_pool_c = _pool_mix(x_pool_c, pool_w[l], pool_b[l], pool_scale[l])
            y_c = _merge((o_mla_c, o_s5_c, o_lru_c, o_pool_c), gpath_c, mlogit_c, w_branch[l], w_out[l])
            xc = xc + gate_c * y_c
        x = x + gate[:, None] * y
    return x
```

```python
import functools
import math

import jax
import jax.numpy as jnp
from jax import lax
from jax.experimental import pallas as pl
from jax.experimental.pallas import tpu as pltpu

F32 = jnp.float32
BF16 = jnp.bfloat16

EPS = 1e-6
GRID_W = 64
N_BRANCH = 4
HEADS = 4
NOPE = 64
ROPE = 32
VDIM = 64
QK = NOPE + ROPE
ROPE_THETA = 10000.0
S5_GROUP = 16
S5_STATE = 64
LRU_BLOCKS = 4
LRU_CONV = 4
LRU_C = 8.0
POOL_WINDOWS = (2, 4, 8, 16)

LANE = 128
HEAD_PAD = LANE
VMEM_LIMIT = 56 * 1024 * 1024

ROW_TILE = 512
Q_TILE = 256
SCAN_CHUNK = 64
SCAN_LANE_SPLITS = 2


def _dot(a, b):
    return jnp.dot(a, b, preferred_element_type=F32)


def _dot_nt(a, b):
    return lax.dot_general(a, b, (((1,), (1,)), ((), ())), preferred_element_type=F32)


def _rms(x, g):
    return x * lax.rsqrt(jnp.mean(x * x, axis=-1, keepdims=True) + EPS) * g


def _const_spec(shape):
    zeros = (0,) * len(shape)
    return pl.BlockSpec(shape, lambda *_: zeros, pipeline_mode=pl.Buffered(1))


def _params(sem):
    return pltpu.CompilerParams(dimension_semantics=sem, vmem_limit_bytes=VMEM_LIMIT)


def _ada_kernel(c_ref, w_ref, b_ref, o_ref):
    c = c_ref[...]
    act = c * jax.nn.sigmoid(c)
    o_ref[0] = jnp.dot(act, w_ref[0], preferred_element_type=F32,
                       precision=lax.Precision.HIGHEST) + b_ref[0]


def _ada(cc, w_ada, b_ada):
    depth, d, d3 = w_ada.shape
    rows = cc.shape[0]
    col = d
    return pl.pallas_call(
        _ada_kernel,
        out_shape=jax.ShapeDtypeStruct((depth, rows, d3), F32),
        grid=(depth, d3 // col),
        in_specs=[pl.BlockSpec((rows, d), lambda l, j: (0, 0)),
                  pl.BlockSpec((1, d, col), lambda l, j: (l, 0, j)),
                  pl.BlockSpec((1, 1, col), lambda l, j: (l, 0, j))],
        out_specs=pl.BlockSpec((1, rows, col), lambda l, j: (l, 0, j)),
        compiler_params=_params(("arbitrary", "arbitrary")),
        name="ada_mod",
    )(cc, w_ada, b_ada.reshape(depth, 1, d3))


def _s5_disc_kernel(are_ref, aim_ref, ldt_ref, bre_ref, bim_ref, abre_ref, abim_ref, bbre_ref, bbim_ref):
    a_re = are_ref[...]
    a_im = aim_ref[...]
    dt = jnp.exp(ldt_ref[...])
    mag = jnp.exp(a_re * dt)
    ab_re = mag * jnp.cos(a_im * dt)
    ab_im = mag * jnp.sin(a_im * dt)
    den = a_re * a_re + a_im * a_im
    f_re = ((ab_re - 1.0) * a_re + ab_im * a_im) / den
    f_im = (ab_im * a_re - (ab_re - 1.0) * a_im) / den
    b_re = bre_ref[...]
    b_im = bim_ref[...]
    abre_ref[...] = ab_re
    abim_ref[...] = ab_im
    bbre_ref[...] = f_re * b_re - f_im * b_im
    bbim_ref[...] = f_re * b_im + f_im * b_re


def _s5_discretize(a_re, a_im, log_dt, b_re, b_im):
    full = b_re.shape
    n = b_re.size
    rows = n // LANE
    expand = lambda a: jnp.broadcast_to(a[..., None], full).reshape(rows, LANE)
    ins = [expand(a_re), expand(a_im), expand(jnp.broadcast_to(log_dt[..., None], a_re.shape)),
           b_re.reshape(rows, LANE), b_im.reshape(rows, LANE)]
    outs = pl.pallas_call(
        _s5_disc_kernel,
        out_shape=[jax.ShapeDtypeStruct((rows, LANE), F32)] * 4,
        name="s5_discretize",
    )(*ins)
    ab_re, ab_im, bb_re, bb_im = [o.reshape(full) for o in outs]
    return ab_re[..., 0], ab_im[..., 0], bb_re, bb_im


def _proj_kernel(x_ref, mod_ref, ng_ref, w1_ref, kvg_ref, wk_ref, wv_ref, qg_ref, wq_ref, wqp_ref,
                 qgain_ref, qgainp_ref, kgain_ref, kgainp_ref, cos_ref, sin_ref,
                 q_out, k_out, vt_out, u_out, xp_out, *, kv_lora, q_lora, branch_w):
    x = x_ref[0]
    mod = mod_ref[0]
    shift, scale = mod[0:1], mod[1:2]
    h = _rms(x, ng_ref[...]) * (1.0 + scale) + shift
    z = _dot(h.astype(BF16), w1_ref[...])

    o = 0
    ckv = z[:, o:o + kv_lora]; o += kv_lora
    kr = z[:, o:o + HEAD_PAD]; o += HEAD_PAD
    kp = z[:, o:o + HEAD_PAD]; o += HEAD_PAD
    cq = z[:, o:o + q_lora]; o += q_lora
    u_out[0] = z[:, o:o + branch_w]; o += branch_w
    xp_out[0] = z[:, o:o + 2 * branch_w]

    cos = cos_ref[...]
    sin = sin_ref[...]
    inv_qk = 1.0 / QK

    cqn = _rms(cq, qg_ref[...]).astype(BF16)
    qraw = _dot(cqn, wq_ref[...])
    qprt = _dot(cqn, wqp_ref[...])
    qgain, qgainp = qgain_ref[...], qgainp_ref[...]
    q_heads = []
    for hh in range(HEADS):
        sl = slice(hh * HEAD_PAD, (hh + 1) * HEAD_PAD)
        qh = qraw[:, sl]
        rs = lax.rsqrt(jnp.sum(qh * qh, axis=-1, keepdims=True) * inv_qk + EPS)
        qn = qh * rs * qgain
        qpn = qprt[:, sl] * rs * qgainp
        q_heads.append(((qn * cos + qpn * sin) * (QK ** -0.5)).astype(BF16))
    q_out[0] = jnp.concatenate(q_heads, axis=-1)

    ckvn = _rms(ckv, kvg_ref[...]).astype(BF16)
    knope = _dot(ckvn, wk_ref[...])
    v = _dot(ckvn, wv_ref[...])
    kgain, kgainp = kgain_ref[...], kgainp_ref[...]
    k_heads = []
    for hh in range(HEADS):
        sl = slice(hh * HEAD_PAD, (hh + 1) * HEAD_PAD)
        kh = knope[:, sl] + kr
        rs = lax.rsqrt(jnp.sum(kh * kh, axis=-1, keepdims=True) * inv_qk + EPS)
        kn = kh * rs * kgain
        kpn = kp * rs * kgainp
        k_heads.append((kn * cos + kpn * sin).astype(BF16))
    k_out[0] = jnp.concatenate(k_heads, axis=-1)
    vt_out[0] = v.T.astype(BF16)


def _proj(x, mod, w, cos_t, sin_t, tm):
    b, t, d = x.shape
    nt = t // tm
    n1 = w["w1"].shape[1]
    bw = w["branch_w"]
    per_batch_mod = mod.shape[0] == b
    mod_map = (lambda i, j: (i, 0, 0)) if per_batch_mod else (lambda i, j: (0, 0, 0))
    hp = HEADS * HEAD_PAD
    kern = functools.partial(_proj_kernel, kv_lora=w["kv_lora"], q_lora=w["q_lora"], branch_w=bw)
    return pl.pallas_call(
        kern,
        out_shape=[jax.ShapeDtypeStruct((b, t, hp), BF16),
                   jax.ShapeDtypeStruct((b, t, hp), BF16),
                   jax.ShapeDtypeStruct((b, HEADS * VDIM, t), BF16),
                   jax.ShapeDtypeStruct((b, t, bw), F32),
                   jax.ShapeDtypeStruct((b, t, 2 * bw), F32)],
        grid=(b, nt),
        in_specs=[pl.BlockSpec((1, tm, d), lambda i, j: (i, j, 0)),
                  pl.BlockSpec((1, 3, d), mod_map),
                  _const_spec((1, d)),
                  _const_spec((d, n1)),
                  _const_spec((1, w["kv_lora"])),
                  _const_spec(w["wk"].shape),
                  _const_spec(w["wv"].shape),
                  _const_spec((1, w["q_lora"])),
                  _const_spec(w["wq"].shape),
                  _const_spec(w["wqp"].shape),
                  _const_spec((1, HEAD_PAD)), _const_spec((1, HEAD_PAD)),
                  _const_spec((1, HEAD_PAD)), _const_spec((1, HEAD_PAD)),
                  pl.BlockSpec((tm, HEAD_PAD), lambda i, j: (j, 0)),
                  pl.BlockSpec((tm, HEAD_PAD), lambda i, j: (j, 0))],
        out_specs=[pl.BlockSpec((1, tm, hp), lambda i, j: (i, j, 0)),
                   pl.BlockSpec((1, tm, hp), lambda i, j: (i, j, 0)),
                   pl.BlockSpec((1, HEADS * VDIM, tm), lambda i, j: (i, 0, j)),
                   pl.BlockSpec((1, tm, bw), lambda i, j: (i, j, 0)),
                   pl.BlockSpec((1, tm, 2 * bw), lambda i, j: (i, j, 0))],
        compiler_params=_params(("arbitrary", "arbitrary")),
        name="proj_qkv",
    )(x, mod, w["norm_g"], w["w1"], w["kv_norm"], w["wk"], w["wv"], w["q_norm"], w["wq"], w["wqp"],
      w["qgain"], w["qgainp"], w["kgain"], w["kgainp"], cos_t, sin_t)


def _seq_kernel(xp_ref, cw_ref, cb_ref, pw_ref, pb_ref, ps_ref, xl_out, op_out, pad_ref, *, t, bw):
    halo = 8
    zeros = jnp.zeros((halo, 2 * bw), F32)
    pad_ref[0:halo, :] = zeros
    pad_ref[halo + t:halo + t + halo, :] = zeros
    pad_ref[halo:halo + t, :] = xp_ref[0]

    def shifted(s, lo, hi):
        return pad_ref[halo + s:halo + s + t, lo:hi]

    left = LRU_CONV // 2
    cw = cw_ref[...]
    acc = cb_ref[...] + shifted(-left, 0, bw) * cw[0:1]
    for k in range(1, LRU_CONV):
        acc = acc + shifted(k - left, 0, bw) * cw[k:k + 1]
    xl_out[0] = acc

    xc = shifted(0, bw, 2 * bw)
    lane = lax.broadcasted_iota(jnp.int32, (t, bw), 1)
    row = lax.broadcasted_iota(jnp.int32, (t, bw), 0)
    group_w = bw // len(POOL_WINDOWS)
    s = shifted(-1, bw, 2 * bw) + xc
    sel = s
    half = jnp.ones((t, bw), jnp.int32)
    for gi in range(1, len(POOL_WINDOWS)):
        hw_prev = POOL_WINDOWS[gi - 1] // 2
        hw = POOL_WINDOWS[gi] // 2
        for k in range(hw_prev, hw):
            s = s + shifted(-(k + 1), bw, 2 * bw) + shifted(k, bw, 2 * bw)
        in_group = lane >= gi * group_w
        sel = jnp.where(in_group, s, sel)
        half = jnp.where(in_group, hw, half)
    cnt = (jnp.minimum(row + half, t) - jnp.maximum(row - half, 0)).astype(F32)
    p = sel / cnt - xc
    y = _dot(p.astype(BF16), pw_ref[...]) + pb_ref[...]
    op_out[0] = y * ps_ref[...]


def _seq(xp, w):
    b, t, w2 = xp.shape
    bw = w2 // 2
    kern = functools.partial(_seq_kernel, t=t, bw=bw)
    return pl.pallas_call(
        kern,
        out_shape=[jax.ShapeDtypeStruct((b, t, bw), F32), jax.ShapeDtypeStruct((b, t, bw), F32)],
        grid=(b,),
        in_specs=[pl.BlockSpec((1, t, w2), lambda i: (i, 0, 0)),
                  _const_spec((LRU_CONV, bw)), _const_spec((1, bw)),
                  _const_spec((bw, bw)), _const_spec((1, bw)), _const_spec((1, bw))],
        out_specs=[pl.BlockSpec((1, t, bw), lambda i: (i, 0, 0)),
                   pl.BlockSpec((1, t, bw), lambda i: (i, 0, 0))],
        scratch_shapes=[pltpu.VMEM((t + 16, w2), F32)],
        compiler_params=_params(("arbitrary",)),
        name="conv_pool",
    )(xp, w["conv_w"], w["conv_b"], w["pool_w"], w["pool_b"], w["pool_scale"])


def _attn_kernel(*refs, nseg):
    q_ref = refs[0]
    k_refs = refs[1:1 + nseg]
    vt_refs = refs[1 + nseg:1 + 2 * nseg]
    o_ref = refs[1 + 2 * nseg]
    outs = []
    for hh in range(HEADS):
        qh = q_ref[0, :, hh * HEAD_PAD:(hh + 1) * HEAD_PAD]
        scores = [_dot_nt(k_ref[0, :, hh * HEAD_PAD:(hh + 1) * HEAD_PAD], qh) for k_ref in k_refs]
        m = jnp.max(scores[0], axis=0, keepdims=True)
        for sc in scores[1:]:
            m = jnp.maximum(m, jnp.max(sc, axis=0, keepdims=True))
        den = None
        acc = None
        for sc, vt_ref in zip(scores, vt_refs):
            e = jnp.exp(sc - m)
            dsum = jnp.sum(e, axis=0, keepdims=True)
            pv = _dot(vt_ref[0, hh * VDIM:(hh + 1) * VDIM, :], e.astype(BF16))
            den = dsum if den is None else den + dsum
            acc = pv if acc is None else acc + pv
        outs.append(acc / den)
    o_ref[0] = jnp.concatenate(outs, axis=0).T


def _attend(q, ks, vts, tq):
    b, t, hp = q.shape
    nseg = len(ks)
    hv = HEADS * VDIM
    in_specs = [pl.BlockSpec((1, tq, hp), lambda i, j: (i, j, 0))]
    for k in ks:
        in_specs.append(pl.BlockSpec((1, k.shape[1], hp), lambda i, j: (i, 0, 0)))
    for vt in vts:
        in_specs.append(pl.BlockSpec((1, hv, vt.shape[2]), lambda i, j: (i, 0, 0)))
    return pl.pallas_call(
        functools.partial(_attn_kernel, nseg=nseg),
        out_shape=jax.ShapeDtypeStruct((b, t, hv), F32),
        grid=(b, t // tq),
        in_specs=in_specs,
        out_specs=pl.BlockSpec((1, tq, hv), lambda i, j: (i, j, 0)),
        compiler_params=_params(("arbitrary", "arbitrary")),
        name="attention",
    )(q, *ks, *vts)


def _scan_kernel(*refs, reverse, accumulate, tc, bsz, bw, ns):
    if accumulate:
        s_ref, prev_ref = refs[0], refs[1]
        refs = refs[2:]
    else:
        s_ref, prev_ref = refs[0], None
        refs = refs[1:]
    (bmat_ref, cre_ref, cim_ref, ar_ref, ai_ref, wa_ref, ba_ref, wx_ref, bx_ref, lam_ref,
     out_ref, hbuf, abuf, bbuf, lbuf, hr_s, hi_s, hl_s) = refs
    r = tc * bsz

    @pl.when(pl.program_id(0) == 0)
    def _():
        hr_s[...] = jnp.zeros_like(hr_s)
        hi_s[...] = jnp.zeros_like(hi_s)
        hl_s[...] = jnp.zeros_like(hl_s)

    blk = s_ref[...]
    u = blk[:, :, 0:bw].reshape(r, bw)
    xl = blk[:, :, bw:2 * bw].reshape(r, bw)
    hbuf[...] = _dot(u.astype(BF16), bmat_ref[...])

    xb = xl.astype(BF16)
    rg = jax.nn.sigmoid(_dot(xb, wa_ref[...]) + ba_ref[...])
    ig = jax.nn.sigmoid(_dot(xb, wx_ref[...]) + bx_ref[...])
    nl = -lam_ref[...]
    softplus = jnp.maximum(nl, 0.0) + jnp.log1p(jnp.exp(-jnp.abs(nl)))
    log_a = (-LRU_C) * rg * softplus
    abuf[...] = jnp.exp(log_a)
    th = jnp.tanh(log_a)
    bbuf[...] = jnp.sqrt(-2.0 * th / (1.0 - th)) * (ig * xl)

    def rows(t):
        te = (tc - 1 - t) if reverse else t
        return pl.ds(pl.multiple_of(te * bsz, bsz), bsz)

    def lru_body(t, hl):
        rw = rows(t)
        hl = abuf[rw, :] * hl + bbuf[rw, :]
        lbuf[rw, :] = hl
        return hl

    hl_s[...] = lax.fori_loop(0, tc, lru_body, hl_s[...])

    w = ns // SCAN_LANE_SPLITS
    for j in range(SCAN_LANE_SPLITS):
        re_sl = slice(j * w, (j + 1) * w)
        im_sl = slice(ns + j * w, ns + (j + 1) * w)
        ar = jnp.broadcast_to(ar_ref[:, re_sl], (bsz, w))
        ai = jnp.broadcast_to(ai_ref[:, re_sl], (bsz, w))

        def s5_body(t, carry, re_sl=re_sl, im_sl=im_sl, ar=ar, ai=ai):
            hr, hi = carry
            rw = rows(t)
            nhr = ar * hr - ai * hi + hbuf[rw, re_sl]
            nhi = ar * hi + ai * hr + hbuf[rw, im_sl]
            hbuf[rw, re_sl] = nhr
            hbuf[rw, im_sl] = nhi
            return nhr, nhi

        hr, hi = lax.fori_loop(0, tc, s5_body, (hr_s[:, re_sl], hi_s[:, re_sl]))
        hr_s[:, re_sl] = hr
        hi_s[:, re_sl] = hi

    y = _dot(hbuf[:, 0:ns].astype(BF16), cre_ref[...]) - _dot(hbuf[:, ns:2 * ns].astype(BF16), cim_ref[...])
    y = y.reshape(tc, bsz, bw)
    hl_all = lbuf[...].reshape(tc, bsz, bw)
    if accumulate:
        prev = prev_ref[...]
        y = y + prev[:, :, 0:bw]
        hl_all = hl_all + prev[:, :, bw:2 * bw]
    out_ref[:, :, 0:bw] = y
    out_ref[:, :, bw:2 * bw] = hl_all


def _scan(s_tm, prev, w, n_lat, n_ctx, reverse):
    t, bsz, w2 = s_tm.shape
    bw = w2 // 2
    tc = SCAN_CHUNK
    ncl, ncc = n_lat // tc, n_ctx // tc
    nc = ncl + ncc
    ns = w["ar"].shape[1]

    if reverse:
        chunk = lambda s: jnp.where(s < ncc, ncl + ncc - 1 - s, nc - 1 - s)
    else:
        chunk = lambda s: jnp.where(s < ncc, ncl + s, s - ncc)
    blk = pl.BlockSpec((tc, bsz, w2), lambda s: (chunk(s), 0, 0))
    accumulate = prev is not None
    kern = functools.partial(_scan_kernel, reverse=reverse, accumulate=accumulate, tc=tc, bsz=bsz,
                             bw=bw, ns=ns)
    weights = [w["bmat"], w["cre"], w["cim"], w["ar"], w["ai"], w["wa"], w["ba"], w["wx"], w["bx"], w["lam"]]
    in_specs = [blk] + ([blk] if accumulate else []) + [_const_spec(a.shape) for a in weights]
    args = [s_tm] + ([prev] if accumulate else []) + weights
    r = tc * bsz
    return pl.pallas_call(
        kern,
        out_shape=jax.ShapeDtypeStruct((t, bsz, w2), F32),
        grid=(nc,),
        in_specs=in_specs,
        out_specs=blk,
        scratch_shapes=[pltpu.VMEM((r, 2 * ns), F32), pltpu.VMEM((r, bw), F32), pltpu.VMEM((r, bw), F32),
                        pltpu.VMEM((r, bw), F32), pltpu.VMEM((bsz, ns), F32), pltpu.VMEM((bsz, ns), F32),
                        pltpu.VMEM((bsz, bw), F32)],
        input_output_aliases=({1: 0} if accumulate else {}),
        compiler_params=_params(("arbitrary",)),
        name="scan_bwd" if reverse else "scan_fwd",
    )(*args)


def _merge_kernel(x_ref, mod_ref, ng_ref, om_ref, r_ref, u_ref, op_ref, w2_ref, wb_ref, wo_ref,
                  wglu_ref, d_ref, out_ref, *, bw, d_model):
    x = x_ref[0]
    mod = mod_ref[0]
    shift, scale, gate = mod[0:1], mod[1:2], mod[2:3]
    h = (_rms(x, ng_ref[...]) * (1.0 + scale) + shift).astype(BF16)

    rr = r_ref[0]
    g = jax.nn.gelu(rr[:, 0:bw] + d_ref[...] * u_ref[0], approximate=True)
    o_s5 = g * jax.nn.sigmoid(_dot(g.astype(BF16), wglu_ref[...]))
    branches = (om_ref[0], o_s5, rr[:, bw:2 * bw], op_ref[0])

    y = None
    for n in range(N_BRANCH):
        gp = _dot(h, w2_ref[:, n * bw:(n + 1) * bw])
        off = N_BRANCH * bw + n * d_model
        ml = _dot(h, w2_ref[:, off:off + d_model])
        tn = _dot((branches[n] * (gp * jax.nn.sigmoid(gp))).astype(BF16), wb_ref[n])
        term = jax.nn.sigmoid(ml) * tn
        y = term if y is None else y + term
    out_ref[0] = x + gate * _dot(y.astype(BF16), wo_ref[...])


def _merge(x, mod, o_mla, r_bm, r_off, u, o_pool, w, tm):
    b, t, d = x.shape
    bw = u.shape[2]
    per_batch_mod = mod.shape[0] == b
    mod_map = (lambda i, j: (i, 0, 0)) if per_batch_mod else (lambda i, j: (0, 0, 0))
    roff = r_off // tm
    kern = functools.partial(_merge_kernel, bw=bw, d_model=d)
    row = lambda width: pl.BlockSpec((1, tm, width), lambda i, j: (i, j, 0))
    return pl.pallas_call(
        kern,
        out_shape=jax.ShapeDtypeStruct((b, t, d), F32),
        grid=(b, t // tm),
        in_specs=[row(d),
                  pl.BlockSpec((1, 3, d), mod_map),
                  _const_spec((1, d)),
                  row(bw),
                  pl.BlockSpec((1, tm, 2 * bw), lambda i, j: (i, j + roff, 0)),
                  row(bw), row(bw),
                  _const_spec(w["w2"].shape), _const_spec(w["wb"].shape), _const_spec(w["wo"].shape),
                  _const_spec(w["wglu"].shape), _const_spec((1, bw))],
        out_specs=row(d),
        compiler_params=_params(("arbitrary", "arbitrary")),
        name="merge_out",
    )(x, mod, w["norm_g"], o_mla, r_bm, u, o_pool, w["w2"], w["wb"], w["wo"], w["wglu"], w["s5_d"])


def _block_diag(blocks):
    n, a, b = blocks.shape
    eye = jnp.eye(n, dtype=blocks.dtype)
    return jnp.einsum("nab,nm->namb", blocks, eye).reshape(n * a, n * b)


def _rope_perm():
    half = ROPE // 2
    nf = half // 2
    base = jnp.concatenate([jnp.arange(nf, half), jnp.arange(0, nf)])
    return jnp.concatenate([base, base + half])


def _head_pad(a, lo):
    n = a.shape[-1]
    pad = [(0, 0)] * (a.ndim - 1) + [(lo, HEAD_PAD - lo - n)]
    return jnp.pad(a, pad)


def _rope_tables(l, with_rope):
    ones = jnp.ones((l, NOPE), F32)
    zeros = jnp.zeros((l, NOPE), F32)
    tail_one = jnp.ones((l, HEAD_PAD - QK), F32)
    tail_zero = jnp.zeros((l, HEAD_PAD - QK), F32)
    if not with_rope:
        return (jnp.concatenate([ones, jnp.ones((l, ROPE), F32), tail_one], -1),
                jnp.concatenate([zeros, jnp.zeros((l, ROPE), F32), tail_zero], -1))
    rows_n = l // GRID_W
    row = jnp.repeat(jnp.arange(rows_n, dtype=jnp.int32), GRID_W).astype(F32)
    col = jnp.tile(jnp.arange(GRID_W, dtype=jnp.int32), rows_n).astype(F32)
    nf = ROPE // 4
    inv = ROPE_THETA ** (-jnp.arange(nf, dtype=F32) / nf)
    ang_r = row[:, None] * inv
    ang_c = col[:, None] * inv
    cr, sr, cc, sc = jnp.cos(ang_r), jnp.sin(ang_r), jnp.cos(ang_c), jnp.sin(ang_c)
    cos = jnp.concatenate([ones, cr, cr, cc, cc, tail_one], -1)
    sin = jnp.concatenate([zeros, -sr, sr, -sc, sc, tail_zero], -1)
    return cos, sin


def _layer_weights(l, p, disc):
    d = p["w_in"].shape[1]
    bw = d // N_BRANCH
    kv_lora = p["mla_kv_norm"].shape[1]
    q_lora = p["mla_q_norm"].shape[1]
    perm = _rope_perm()
    w_in = p["w_in"][l]
    off_krope = kv_lora
    off_s5 = off_krope + ROPE
    off_lru = off_s5 + bw
    off_cq = off_lru + bw
    off_pool = off_cq + q_lora
    off_gate = off_pool + bw
    w_krope = w_in[:, off_krope:off_s5]
    w1 = jnp.concatenate([
        w_in[:, :kv_lora],
        _head_pad(w_krope, NOPE),
        _head_pad(w_krope[:, perm], NOPE),
        w_in[:, off_cq:off_pool],
        w_in[:, off_s5:off_lru],
        w_in[:, off_lru:off_cq],
        w_in[:, off_pool:off_gate]], axis=1).astype(BF16)
    w2 = w_in[:, off_gate:].astype(BF16)

    w_uq = p["mla_w_uq"][l].reshape(q_lora, HEADS, QK)
    wq = _head_pad(w_uq, 0).reshape(q_lora, HEADS * HEAD_PAD).astype(BF16)
    wqp = _head_pad(w_uq[:, :, NOPE:][:, :, perm], NOPE).reshape(q_lora, HEADS * HEAD_PAD).astype(BF16)
    w_ukv = p["mla_w_ukv"][l].reshape(kv_lora, HEADS, NOPE + VDIM)
    wk = _head_pad(w_ukv[:, :, :NOPE], 0).reshape(kv_lora, HEADS * HEAD_PAD).astype(BF16)
    wv = w_ukv[:, :, NOPE:].reshape(kv_lora, HEADS * VDIM).astype(BF16)
    q_gain, k_gain = p["mla_q_gain"][l], p["mla_k_gain"][l]

    ab_re, ab_im, bb_re, bb_im = disc
    scan = []
    for dr in range(2):
        bmat = jnp.concatenate([
            _block_diag(jnp.swapaxes(bb_re[l, dr], 1, 2)),
            _block_diag(jnp.swapaxes(bb_im[l, dr], 1, 2))], axis=1).astype(BF16)
        scan.append(dict(
            bmat=bmat,
            cre=_block_diag(jnp.swapaxes(p["s5_c_re"][l, dr], 1, 2)).astype(BF16),
            cim=_block_diag(jnp.swapaxes(p["s5_c_im"][l, dr], 1, 2)).astype(BF16),
            ar=ab_re[l, dr].reshape(1, -1), ai=ab_im[l, dr].reshape(1, -1),
            wa=_block_diag(p["lru_w_a"][l, dr]).astype(BF16), ba=p["lru_b_a"][l, dr][None],
            wx=_block_diag(p["lru_w_x"][l, dr]).astype(BF16), bx=p["lru_b_x"][l, dr][None],
            lam=p["lru_lambda"][l, dr][None]))

    return dict(
        branch_w=bw, kv_lora=kv_lora, q_lora=q_lora,
        norm_g=p["norm_g"][l][None], w1=w1, w2=w2,
        kv_norm=p["mla_kv_norm"][l][None], q_norm=p["mla_q_norm"][l][None],
        wq=wq, wqp=wqp, wk=wk, wv=wv,
        qgain=_head_pad(q_gain, 0)[None], qgainp=_head_pad(q_gain[NOPE:][perm], NOPE)[None],
        kgain=_head_pad(k_gain, 0)[None], kgainp=_head_pad(k_gain[NOPE:][perm], NOPE)[None],
        conv_w=p["lru_conv_w"][l], conv_b=p["lru_conv_b"][l][None],
        pool_w=_block_diag(p["pool_w"][l]).astype(BF16), pool_b=p["pool_b"][l][None],
        pool_scale=p["pool_scale"][l][None],
        wb=p["w_branch"][l].astype(BF16), wo=p["w_out"][l].astype(BF16),
        wglu=p["s5_w_glu"][l].astype(BF16), s5_d=p["s5_d"][l][None],
        scan=scan)


def kernel(x, c, ctx, c_ctx, w_ada, b_ada, norm_g, w_in, mla_q_norm, mla_kv_norm, mla_w_uq, mla_w_ukv, mla_q_gain, mla_k_gain, s5_a_re, s5_a_im, s5_log_dt, s5_b_re, s5_b_im, s5_c_re, s5_c_im, s5_d, s5_w_glu, lru_conv_w, lru_conv_b, lru_lambda, lru_w_a, lru_b_a, lru_w_x, lru_b_x, pool_w, pool_b, pool_scale, w_branch, w_out):
    p = dict(w_in=w_in, norm_g=norm_g, mla_q_norm=mla_q_norm, mla_kv_norm=mla_kv_norm,
             mla_w_uq=mla_w_uq, mla_w_ukv=mla_w_ukv, mla_q_gain=mla_q_gain, mla_k_gain=mla_k_gain,
             s5_c_re=s5_c_re, s5_c_im=s5_c_im, s5_d=s5_d, s5_w_glu=s5_w_glu,
             lru_conv_w=lru_conv_w, lru_conv_b=lru_conv_b, lru_lambda=lru_lambda,
             lru_w_a=lru_w_a, lru_b_a=lru_b_a, lru_w_x=lru_w_x, lru_b_x=lru_b_x,
             pool_w=pool_w, pool_b=pool_b, pool_scale=pool_scale, w_branch=w_branch, w_out=w_out)
    depth = w_in.shape[0]
    b, l, d = x.shape
    lc = ctx.shape[1]
    tm_l = min(ROW_TILE, l)
    tm_c = min(ROW_TILE, lc)
    tq_l = min(Q_TILE, l)
    tq_c = min(Q_TILE, lc)

    n_rows = -(-(b + 1) // 8) * 8
    cc = jnp.concatenate([c, c_ctx[None], jnp.zeros((n_rows - b - 1, d), F32)], axis=0)
    mod_all = _ada(cc, w_ada, b_ada)
    disc = _s5_discretize(s5_a_re, s5_a_im, s5_log_dt, s5_b_re, s5_b_im)
    cos_l, sin_l = _rope_tables(l, True)
    cos_c, sin_c = _rope_tables(lc, False)

    xc = ctx
    for li in range(depth):
        with_ctx = li < depth - 1
        w = _layer_weights(li, p, disc)
        mod_l = mod_all[li, :b].reshape(b, 3, d)
        mod_c = mod_all[li, b:b + 1].reshape(1, 3, d)

        q_l, k_l, vt_l, u_l, xp_l = _proj(x, mod_l, w, cos_l, sin_l, tm_l)
        q_c, k_c, vt_c, u_c, xp_c = _proj(xc, mod_c, w, cos_c, sin_c, tm_c)
        xl_l, op_l = _seq(xp_l, w)
        xl_c, op_c = _seq(xp_c, w)

        o_mla = _attend(q_l, [k_c, k_l], [vt_c, vt_l], tq_l)

        s_tm = jnp.concatenate([
            jnp.transpose(jnp.concatenate([u_l, xl_l], axis=-1), (1, 0, 2)),
            jnp.transpose(jnp.concatenate([u_c, xl_c], axis=-1), (1, 0, 2))], axis=0)
        y_tm = _scan(s_tm, None, w["scan"][0], l, lc, reverse=False)
        y_tm = _scan(s_tm, y_tm, w["scan"][1], l, lc, reverse=True)
        r_bm = jnp.transpose(y_tm, (1, 0, 2))

        x_new = _merge(x, mod_l, o_mla, r_bm, 0, u_l, op_l, w, tm_l)
        if with_ctx:
            o_mla_c = _attend(q_c, [k_c], [vt_c], tq_c)
            xc = _merge(xc, mod_c, o_mla_c, r_bm, l, u_c, op_c, w, tm_c)
        x = x_new
    return x
```

```python
import functools
import math

import jax
import jax.numpy as jnp
from jax import lax
from jax.experimental import pallas as pl
from jax.experimental.pallas import tpu as pltpu

F32 = jnp.float32
BF16 = jnp.bfloat16

EPS = 1e-6
GRID_W = 64
N_BRANCH = 4
HEADS = 4
NOPE = 64
ROPE = 32
VDIM = 64
QK = NOPE + ROPE
ROPE_THETA = 10000.0
S5_GROUP = 16
S5_STATE = 64
LRU_BLOCKS = 4
LRU_CONV = 4
LRU_C = 8.0
POOL_WINDOWS = (2, 4, 8, 16)
Q_SCALE = (QK ** -0.5) * math.log2(math.e)

LANE = 128
HEAD_PAD = LANE
VMEM_LIMIT = 56 * 1024 * 1024

ROW_TILE = 512
Q_TILE = 512
SCAN_CHUNK = 64
SCAN_LANE_SPLITS = 2


def _dot(a, b):
    return jnp.dot(a, b, preferred_element_type=F32)


def _dot_nt(a, b):
    return lax.dot_general(a, b, (((1,), (1,)), ((), ())), preferred_element_type=F32)


def _rms(x, g):
    return x * lax.rsqrt(jnp.mean(x * x, axis=-1, keepdims=True) + EPS) * g


def _const_spec(shape):
    zeros = (0,) * len(shape)
    return pl.BlockSpec(shape, lambda *_: zeros, pipeline_mode=pl.Buffered(1))


def _params(sem):
    return pltpu.CompilerParams(dimension_semantics=sem, vmem_limit_bytes=VMEM_LIMIT)


def _ada_kernel(c_ref, w_ref, b_ref, o_ref):
    c = c_ref[...]
    act = c * jax.nn.sigmoid(c)
    o_ref[0] = jnp.dot(act, w_ref[0], preferred_element_type=F32,
                       precision=lax.Precision.HIGHEST) + b_ref[0]


def _ada(cc, w_ada, b_ada):
    depth, d, d3 = w_ada.shape
    rows = cc.shape[0]
    col = d
    return pl.pallas_call(
        _ada_kernel,
        out_shape=jax.ShapeDtypeStruct((depth, rows, d3), F32),
        grid=(depth, d3 // col),
        in_specs=[pl.BlockSpec((rows, d), lambda l, j: (0, 0)),
                  pl.BlockSpec((1, d, col), lambda l, j: (l, 0, j)),
                  pl.BlockSpec((1, 1, col), lambda l, j: (l, 0, j))],
        out_specs=pl.BlockSpec((1, rows, col), lambda l, j: (l, 0, j)),
        compiler_params=_params(("arbitrary", "arbitrary")),
        name="ada_mod",
    )(cc, w_ada, b_ada.reshape(depth, 1, d3))


def _s5_disc_kernel(are_ref, aim_ref, ldt_ref, bre_ref, bim_ref, abre_ref, abim_ref, bbre_ref, bbim_ref):
    a_re = are_ref[...]
    a_im = aim_ref[...]
    dt = jnp.exp(ldt_ref[...])
    mag = jnp.exp(a_re * dt)
    ab_re = mag * jnp.cos(a_im * dt)
    ab_im = mag * jnp.sin(a_im * dt)
    den = a_re * a_re + a_im * a_im
    f_re = ((ab_re - 1.0) * a_re + ab_im * a_im) / den
    f_im = (ab_im * a_re - (ab_re - 1.0) * a_im) / den
    b_re = bre_ref[...]
    b_im = bim_ref[...]
    abre_ref[...] = ab_re
    abim_ref[...] = ab_im
    bbre_ref[...] = f_re * b_re - f_im * b_im
    bbim_ref[...] = f_re * b_im + f_im * b_re


def _s5_discretize(a_re, a_im, log_dt, b_re, b_im):
    full = b_re.shape
    n = b_re.size
    rows = n // LANE
    expand = lambda a: jnp.broadcast_to(a[..., None], full).reshape(rows, LANE)
    ins = [expand(a_re), expand(a_im), expand(jnp.broadcast_to(log_dt[..., None], a_re.shape)),
           b_re.reshape(rows, LANE), b_im.reshape(rows, LANE)]
    outs = pl.pallas_call(
        _s5_disc_kernel,
        out_shape=[jax.ShapeDtypeStruct((rows, LANE), F32)] * 4,
        name="s5_discretize",
    )(*ins)
    ab_re, ab_im, bb_re, bb_im = [o.reshape(full) for o in outs]
    return ab_re[..., 0], ab_im[..., 0], bb_re, bb_im


def _proj_kernel(x_ref, mod_ref, ng_ref, w1_ref, kvg_ref, wk_ref, wv_ref, qg_ref, wq_ref, wqp_ref,
                 qgain_ref, qgainp_ref, kgain_ref, kgainp_ref, cos_ref, sin_ref,
                 q_out, k_out, vt_out, u_out, xp_out, *, kv_lora, q_lora, branch_w):
    x = x_ref[0]
    mod = mod_ref[0]
    shift, scale = mod[0:1], mod[1:2]
    h = _rms(x, ng_ref[...]) * (1.0 + scale) + shift
    z = _dot(h.astype(BF16), w1_ref[...])

    o = 0
    ckv = z[:, o:o + kv_lora]; o += kv_lora
    kr = z[:, o:o + HEAD_PAD]; o += HEAD_PAD
    kp = z[:, o:o + HEAD_PAD]; o += HEAD_PAD
    cq = z[:, o:o + q_lora]; o += q_lora
    u_out[0] = z[:, o:o + branch_w]; o += branch_w
    xp_out[0] = z[:, o:o + 2 * branch_w]

    cos = cos_ref[...]
    sin = sin_ref[...]
    inv_qk = 1.0 / QK

    cqn = _rms(cq, qg_ref[...]).astype(BF16)
    qraw = _dot(cqn, wq_ref[...])
    qprt = _dot(cqn, wqp_ref[...])
    qgain, qgainp = qgain_ref[...], qgainp_ref[...]
    q_heads = []
    for hh in range(HEADS):
        sl = slice(hh * HEAD_PAD, (hh + 1) * HEAD_PAD)
        qh = qraw[:, sl]
        rs = lax.rsqrt(jnp.sum(qh * qh, axis=-1, keepdims=True) * inv_qk + EPS)
        qn = qh * rs * qgain
        qpn = qprt[:, sl] * rs * qgainp
        q_heads.append(((qn * cos + qpn * sin) * Q_SCALE).astype(BF16))
    q_out[0] = jnp.concatenate(q_heads, axis=-1)

    ckvn = _rms(ckv, kvg_ref[...]).astype(BF16)
    knope = _dot(ckvn, wk_ref[...])
    v = _dot(ckvn, wv_ref[...])
    kgain, kgainp = kgain_ref[...], kgainp_ref[...]
    k_heads = []
    for hh in range(HEADS):
        sl = slice(hh * HEAD_PAD, (hh + 1) * HEAD_PAD)
        kh = knope[:, sl] + kr
        rs = lax.rsqrt(jnp.sum(kh * kh, axis=-1, keepdims=True) * inv_qk + EPS)
        kn = kh * rs * kgain
        kpn = kp * rs * kgainp
        k_heads.append((kn * cos + kpn * sin).astype(BF16))
    k_out[0] = jnp.concatenate(k_heads, axis=-1)
    vt_out[0] = v.T.astype(BF16)


def _proj(x, mod, w, cos_t, sin_t, tm):
    b, t, d = x.shape
    nt = t // tm
    n1 = w["w1"].shape[1]
    bw = w["branch_w"]
    per_batch_mod = mod.shape[0] == b
    mod_map = (lambda i, j: (i, 0, 0)) if per_batch_mod else (lambda i, j: (0, 0, 0))
    hp = HEADS * HEAD_PAD
    kern = functools.partial(_proj_kernel, kv_lora=w["kv_lora"], q_lora=w["q_lora"], branch_w=bw)
    return pl.pallas_call(
        kern,
        out_shape=[jax.ShapeDtypeStruct((b, t, hp), BF16),
                   jax.ShapeDtypeStruct((b, t, hp), BF16),
                   jax.ShapeDtypeStruct((b, HEADS * VDIM, t), BF16),
                   jax.ShapeDtypeStruct((b, t, bw), F32),
                   jax.ShapeDtypeStruct((b, t, 2 * bw), F32)],
        grid=(b, nt),
        in_specs=[pl.BlockSpec((1, tm, d), lambda i, j: (i, j, 0)),
                  pl.BlockSpec((1, 3, d), mod_map),
                  _const_spec((1, d)),
                  _const_spec((d, n1)),
                  _const_spec((1, w["kv_lora"])),
                  _const_spec(w["wk"].shape),
                  _const_spec(w["wv"].shape),
                  _const_spec((1, w["q_lora"])),
                  _const_spec(w["wq"].shape),
                  _const_spec(w["wqp"].shape),
                  _const_spec((1, HEAD_PAD)), _const_spec((1, HEAD_PAD)),
                  _const_spec((1, HEAD_PAD)), _const_spec((1, HEAD_PAD)),
                  pl.BlockSpec((tm, HEAD_PAD), lambda i, j: (j, 0)),
                  pl.BlockSpec((tm, HEAD_PAD), lambda i, j: (j, 0))],
        out_specs=[pl.BlockSpec((1, tm, hp), lambda i, j: (i, j, 0)),
                   pl.BlockSpec((1, tm, hp), lambda i, j: (i, j, 0)),
                   pl.BlockSpec((1, HEADS * VDIM, tm), lambda i, j: (i, 0, j)),
                   pl.BlockSpec((1, tm, bw), lambda i, j: (i, j, 0)),
                   pl.BlockSpec((1, tm, 2 * bw), lambda i, j: (i, j, 0))],
        compiler_params=_params(("arbitrary", "arbitrary")),
        name="proj_qkv",
    )(x, mod, w["norm_g"], w["w1"], w["kv_norm"], w["wk"], w["wv"], w["q_norm"], w["wq"], w["wqp"],
      w["qgain"], w["qgainp"], w["kgain"], w["kgainp"], cos_t, sin_t)


def _seq_kernel(xp_ref, cw_ref, cb_ref, pw_ref, pb_ref, ps_ref, xl_out, op_out, pad_ref, *, t, bw):
    halo = 8
    zeros = jnp.zeros((halo, 2 * bw), F32)
    pad_ref[0:halo, :] = zeros
    pad_ref[halo + t:halo + t + halo, :] = zeros
    pad_ref[halo:halo + t, :] = xp_ref[0]

    def shifted(s, lo, hi):
        return pad_ref[halo + s:halo + s + t, lo:hi]

    left = LRU_CONV // 2
    cw = cw_ref[...]
    acc = cb_ref[...] + shifted(-left, 0, bw) * cw[0:1]
    for k in range(1, LRU_CONV):
        acc = acc + shifted(k - left, 0, bw) * cw[k:k + 1]
    xl_out[0] = acc

    xc = shifted(0, bw, 2 * bw)
    lane = lax.broadcasted_iota(jnp.int32, (t, bw), 1)
    row = lax.broadcasted_iota(jnp.int32, (t, bw), 0)
    group_w = bw // len(POOL_WINDOWS)
    s = shifted(-1, bw, 2 * bw) + xc
    sel = s
    half = jnp.ones((t, bw), jnp.int32)
    for gi in range(1, len(POOL_WINDOWS)):
        hw_prev = POOL_WINDOWS[gi - 1] // 2
        hw = POOL_WINDOWS[gi] // 2
        for k in range(hw_prev, hw):
            s = s + shifted(-(k + 1), bw, 2 * bw) + shifted(k, bw, 2 * bw)
        in_group = lane >= gi * group_w
        sel = jnp.where(in_group, s, sel)
        half = jnp.where(in_group, hw, half)
    cnt = (jnp.minimum(row + half, t) - jnp.maximum(row - half, 0)).astype(F32)
    p = sel / cnt - xc
    y = _dot(p.astype(BF16), pw_ref[...]) + pb_ref[...]
    op_out[0] = y * ps_ref[...]


def _seq(xp, w):
    b, t, w2 = xp.shape
    bw = w2 // 2
    kern = functools.partial(_seq_kernel, t=t, bw=bw)
    return pl.pallas_call(
        kern,
        out_shape=[jax.ShapeDtypeStruct((b, t, bw), F32), jax.ShapeDtypeStruct((b, t, bw), F32)],
        grid=(b,),
        in_specs=[pl.BlockSpec((1, t, w2), lambda i: (i, 0, 0)),
                  _const_spec((LRU_CONV, bw)), _const_spec((1, bw)),
                  _const_spec((bw, bw)), _const_spec((1, bw)), _const_spec((1, bw))],
        out_specs=[pl.BlockSpec((1, t, bw), lambda i: (i, 0, 0)),
                   pl.BlockSpec((1, t, bw), lambda i: (i, 0, 0))],
        scratch_shapes=[pltpu.VMEM((t + 16, w2), F32)],
        compiler_params=_params(("arbitrary",)),
        name="conv_pool",
    )(xp, w["conv_w"], w["conv_b"], w["pool_w"], w["pool_b"], w["pool_scale"])


def _attn_kernel(*refs, nseg):
    q_ref = refs[0]
    k_refs = refs[1:1 + nseg]
    vt_refs = refs[1 + nseg:1 + 2 * nseg]
    o_ref = refs[1 + 2 * nseg]
    s_buf = refs[2 + 2 * nseg]
    seg_rows = []
    row = 0
    for k_ref in k_refs:
        seg_rows.append((row, row + k_ref.shape[1]))
        row += k_ref.shape[1]

    def scores(hh, slot):
        qh = q_ref[0, :, hh * HEAD_PAD:(hh + 1) * HEAD_PAD]
        m = None
        for k_ref, (lo, hi) in zip(k_refs, seg_rows):
            sc = _dot_nt(k_ref[0, :, hh * HEAD_PAD:(hh + 1) * HEAD_PAD], qh)
            s_buf[slot, lo:hi, :] = sc
            mm = jnp.max(sc, axis=0, keepdims=True)
            m = mm if m is None else jnp.maximum(m, mm)
        return m

    def weighted_values(hh, slot, m):
        den = None
        acc = None
        for vt_ref, (lo, hi) in zip(vt_refs, seg_rows):
            e = jnp.exp2(s_buf[slot, lo:hi, :] - m)
            dsum = jnp.sum(e, axis=0, keepdims=True)
            pv = _dot(vt_ref[0, hh * VDIM:(hh + 1) * VDIM, :], e.astype(BF16))
            den = dsum if den is None else den + dsum
            acc = pv if acc is None else acc + pv
        return acc / den

    outs = []
    m_cur = scores(0, 0)
    for hh in range(HEADS):
        m_next = scores(hh + 1, (hh + 1) % 2) if hh + 1 < HEADS else None
        outs.append(weighted_values(hh, hh % 2, m_cur))
        m_cur = m_next
    o_ref[0] = jnp.concatenate(outs, axis=0).T


def _attend(q, ks, vts, tq):
    b, t, hp = q.shape
    nseg = len(ks)
    hv = HEADS * VDIM
    in_specs = [pl.BlockSpec((1, tq, hp), lambda i, j: (i, j, 0))]
    for k in ks:
        in_specs.append(pl.BlockSpec((1, k.shape[1], hp), lambda i, j: (i, 0, 0)))
    for vt in vts:
        in_specs.append(pl.BlockSpec((1, hv, vt.shape[2]), lambda i, j: (i, 0, 0)))
    return pl.pallas_call(
        functools.partial(_attn_kernel, nseg=nseg),
        out_shape=jax.ShapeDtypeStruct((b, t, hv), F32),
        grid=(b, t // tq),
        in_specs=in_specs,
        out_specs=pl.BlockSpec((1, tq, hv), lambda i, j: (i, j, 0)),
        scratch_shapes=[pltpu.VMEM((2, sum(k.shape[1] for k in ks), tq), F32)],
        compiler_params=_params(("arbitrary", "arbitrary")),
        name="attention",
    )(q, *ks, *vts)


def _scan_kernel(*refs, reverse, accumulate, tc, bsz, bw, ns):
    if accumulate:
        s_ref, prev_ref = refs[0], refs[1]
        refs = refs[2:]
    else:
        s_ref, prev_ref = refs[0], None
        refs = refs[1:]
    (bmat_ref, cre_ref, cim_ref, ar_ref, ai_ref, wa_ref, ba_ref, wx_ref, bx_ref, lam_ref,
     out_ref, hbuf, abuf, bbuf, lbuf, hr_s, hi_s, hl_s) = refs
    r = tc * bsz

    @pl.when(pl.program_id(0) == 0)
    def _():
        hr_s[...] = jnp.zeros_like(hr_s)
        hi_s[...] = jnp.zeros_like(hi_s)
        hl_s[...] = jnp.zeros_like(hl_s)

    blk = s_ref[...]
    u = blk[:, :, 0:bw].reshape(r, bw)
    xl = blk[:, :, bw:2 * bw].reshape(r, bw)
    hbuf[...] = _dot(u.astype(BF16), bmat_ref[...])

    xb = xl.astype(BF16)
    rg = jax.nn.sigmoid(_dot(xb, wa_ref[...]) + ba_ref[...])
    ig = jax.nn.sigmoid(_dot(xb, wx_ref[...]) + bx_ref[...])
    nl = -lam_ref[...]
    softplus = jnp.maximum(nl, 0.0) + jnp.log1p(jnp.exp(-jnp.abs(nl)))
    log_a = (-LRU_C) * rg * softplus
    abuf[...] = jnp.exp(log_a)
    th = jnp.tanh(log_a)
    bbuf[...] = jnp.sqrt(-2.0 * th / (1.0 - th)) * (ig * xl)

    def rows(t):
        te = (tc - 1 - t) if reverse else t
        return pl.ds(pl.multiple_of(te * bsz, bsz), bsz)

    def lru_body(t, hl):
        rw = rows(t)
        hl = abuf[rw, :] * hl + bbuf[rw, :]
        lbuf[rw, :] = hl
        return hl

    hl_s[...] = lax.fori_loop(0, tc, lru_body, hl_s[...])

    w = ns // SCAN_LANE_SPLITS
    for j in range(SCAN_LANE_SPLITS):
        re_sl = slice(j * w, (j + 1) * w)
        im_sl = slice(ns + j * w, ns + (j + 1) * w)
        ar = jnp.broadcast_to(ar_ref[:, re_sl], (bsz, w))
        ai = jnp.broadcast_to(ai_ref[:, re_sl], (bsz, w))

        def s5_body(t, carry, re_sl=re_sl, im_sl=im_sl, ar=ar, ai=ai):
            hr, hi = carry
            rw = rows(t)
            nhr = ar * hr - ai * hi + hbuf[rw, re_sl]
            nhi = ar * hi + ai * hr + hbuf[rw, im_sl]
            hbuf[rw, re_sl] = nhr
            hbuf[rw, im_sl] = nhi
            return nhr, nhi

        hr, hi = lax.fori_loop(0, tc, s5_body, (hr_s[:, re_sl], hi_s[:, re_sl]))
        hr_s[:, re_sl] = hr
        hi_s[:, re_sl] = hi

    y = _dot(hbuf[:, 0:ns].astype(BF16), cre_ref[...]) - _dot(hbuf[:, ns:2 * ns].astype(BF16), cim_ref[...])
    y = y.reshape(tc, bsz, bw)
    hl_all = lbuf[...].reshape(tc, bsz, bw)
    if accumulate:
        prev = prev_ref[...]
        y = y + prev[:, :, 0:bw]
        hl_all = hl_all + prev[:, :, bw:2 * bw]
    out_ref[:, :, 0:bw] = y
    out_ref[:, :, bw:2 * bw] = hl_all


def _scan(s_tm, prev, w, n_lat, n_ctx, reverse):
    t, bsz, w2 = s_tm.shape
    bw = w2 // 2
    tc = SCAN_CHUNK
    ncl, ncc = n_lat // tc, n_ctx // tc
    nc = ncl + ncc
    ns = w["ar"].shape[1]

    if reverse:
        chunk = lambda s: jnp.where(s < ncc, ncl + ncc - 1 - s, nc - 1 - s)
    else:
        chunk = lambda s: jnp.where(s < ncc, ncl + s, s - ncc)
    blk = pl.BlockSpec((tc, bsz, w2), lambda s: (chunk(s), 0, 0))
    accumulate = prev is not None
    kern = functools.partial(_scan_kernel, reverse=reverse, accumulate=accumulate, tc=tc, bsz=bsz,
                             bw=bw, ns=ns)
    weights = [w["bmat"], w["cre"], w["cim"], w["ar"], w["ai"], w["wa"], w["ba"], w["wx"], w["bx"], w["lam"]]
    in_specs = [blk] + ([blk] if accumulate else []) + [_const_spec(a.shape) for a in weights]
    args = [s_tm] + ([prev] if accumulate else []) + weights
    r = tc * bsz
    return pl.pallas_call(
        kern,
        out_shape=jax.ShapeDtypeStruct((t, bsz, w2), F32),
        grid=(nc,),
        in_specs=in_specs,
        out_specs=blk,
        scratch_shapes=[pltpu.VMEM((r, 2 * ns), F32), pltpu.VMEM((r, bw), F32), pltpu.VMEM((r, bw), F32),
                        pltpu.VMEM((r, bw), F32), pltpu.VMEM((bsz, ns), F32), pltpu.VMEM((bsz, ns), F32),
                        pltpu.VMEM((bsz, bw), F32)],
        input_output_aliases=({1: 0} if accumulate else {}),
        compiler_params=_params(("arbitrary",)),
        name="scan_bwd" if reverse else "scan_fwd",
    )(*args)


def _merge_kernel(x_ref, mod_ref, ng_ref, om_ref, r_ref, u_ref, op_ref, w2_ref, wb_ref, wo_ref,
                  wglu_ref, d_ref, out_ref, *, bw, d_model):
    x = x_ref[0]
    mod = mod_ref[0]
    shift, scale, gate = mod[0:1], mod[1:2], mod[2:3]
    h = (_rms(x, ng_ref[...]) * (1.0 + scale) + shift).astype(BF16)

    rr = r_ref[0]
    g = jax.nn.gelu(rr[:, 0:bw] + d_ref[...] * u_ref[0], approximate=True)
    o_s5 = g * jax.nn.sigmoid(_dot(g.astype(BF16), wglu_ref[...]))
    branches = (om_ref[0], o_s5, rr[:, bw:2 * bw], op_ref[0])

    y = None
    for n in range(N_BRANCH):
        gp = _dot(h, w2_ref[:, n * bw:(n + 1) * bw])
        off = N_BRANCH * bw + n * d_model
        ml = _dot(h, w2_ref[:, off:off + d_model])
        tn = _dot((branches[n] * (gp * jax.nn.sigmoid(gp))).astype(BF16), wb_ref[n])
        term = jax.nn.sigmoid(ml) * tn
        y = term if y is None else y + term
    out_ref[0] = x + gate * _dot(y.astype(BF16), wo_ref[...])


def _merge(x, mod, o_mla, r_bm, r_off, u, o_pool, w, tm):
    b, t, d = x.shape
    bw = u.shape[2]
    per_batch_mod = mod.shape[0] == b
    mod_map = (lambda i, j: (i, 0, 0)) if per_batch_mod else (lambda i, j: (0, 0, 0))
    roff = r_off // tm
    kern = functools.partial(_merge_kernel, bw=bw, d_model=d)
    row = lambda width: pl.BlockSpec((1, tm, width), lambda i, j: (i, j, 0))
    return pl.pallas_call(
        kern,
        out_shape=jax.ShapeDtypeStruct((b, t, d), F32),
        grid=(b, t // tm),
        in_specs=[row(d),
                  pl.BlockSpec((1, 3, d), mod_map),
                  _const_spec((1, d)),
                  row(bw),
                  pl.BlockSpec((1, tm, 2 * bw), lambda i, j: (i, j + roff, 0)),
                  row(bw), row(bw),
                  _const_spec(w["w2"].shape), _const_spec(w["wb"].shape), _const_spec(w["wo"].shape),
                  _const_spec(w["wglu"].shape), _const_spec((1, bw))],
        out_specs=row(d),
        compiler_params=_params(("arbitrary", "arbitrary")),
        name="merge_out",
    )(x, mod, w["norm_g"], o_mla, r_bm, u, o_pool, w["w2"], w["wb"], w["wo"], w["wglu"], w["s5_d"])


def _block_diag(blocks):
    n, a, b = blocks.shape
    eye = jnp.eye(n, dtype=blocks.dtype)
    return jnp.einsum("nab,nm->namb", blocks, eye).reshape(n * a, n * b)


def _rope_perm():
    half = ROPE // 2
    nf = half // 2
    base = jnp.concatenate([jnp.arange(nf, half), jnp.arange(0, nf)])
    return jnp.concatenate([base, base + half])


def _head_pad(a, lo):
    n = a.shape[-1]
    pad = [(0, 0)] * (a.ndim - 1) + [(lo, HEAD_PAD - lo - n)]
    return jnp.pad(a, pad)


def _rope_tables(l, with_rope):
    ones = jnp.ones((l, NOPE), F32)
    zeros = jnp.zeros((l, NOPE), F32)
    tail_one = jnp.ones((l, HEAD_PAD - QK), F32)
    tail_zero = jnp.zeros((l, HEAD_PAD - QK), F32)
    if not with_rope:
        return (jnp.concatenate([ones, jnp.ones((l, ROPE), F32), tail_one], -1),
                jnp.concatenate([zeros, jnp.zeros((l, ROPE), F32), tail_zero], -1))
    rows_n = l // GRID_W
    row = jnp.repeat(jnp.arange(rows_n, dtype=jnp.int32), GRID_W).astype(F32)
    col = jnp.tile(jnp.arange(GRID_W, dtype=jnp.int32), rows_n).astype(F32)
    nf = ROPE // 4
    inv = ROPE_THETA ** (-jnp.arange(nf, dtype=F32) / nf)
    ang_r = row[:, None] * inv
    ang_c = col[:, None] * inv
    cr, sr, cc, sc = jnp.cos(ang_r), jnp.sin(ang_r), jnp.cos(ang_c), jnp.sin(ang_c)
    cos = jnp.concatenate([ones, cr, cr, cc, cc, tail_one], -1)
    sin = jnp.concatenate([zeros, -sr, sr, -sc, sc, tail_zero], -1)
    return cos, sin


def _layer_weights(l, p, disc):
    d = p["w_in"].shape[1]
    bw = d // N_BRANCH
    kv_lora = p["mla_kv_norm"].shape[1]
    q_lora = p["mla_q_norm"].shape[1]
    perm = _rope_perm()
    w_in = p["w_in"][l]
    off_krope = kv_lora
    off_s5 = off_krope + ROPE
    off_lru = off_s5 + bw
    off_cq = off_lru + bw
    off_pool = off_cq + q_lora
    off_gate = off_pool + bw
    w_krope = w_in[:, off_krope:off_s5]
    w1 = jnp.concatenate([
        w_in[:, :kv_lora],
        _head_pad(w_krope, NOPE),
        _head_pad(w_krope[:, perm], NOPE),
        w_in[:, off_cq:off_pool],
        w_in[:, off_s5:off_lru],
        w_in[:, off_lru:off_cq],
        w_in[:, off_pool:off_gate]], axis=1).astype(BF16)
    w2 = w_in[:, off_gate:].astype(BF16)

    w_uq = p["mla_w_uq"][l].reshape(q_lora, HEADS, QK)
    wq = _head_pad(w_uq, 0).reshape(q_lora, HEADS * HEAD_PAD).astype(BF16)
    wqp = _head_pad(w_uq[:, :, NOPE:][:, :, perm], NOPE).reshape(q_lora, HEADS * HEAD_PAD).astype(BF16)
    w_ukv = p["mla_w_ukv"][l].reshape(kv_lora, HEADS, NOPE + VDIM)
    wk = _head_pad(w_ukv[:, :, :NOPE], 0).reshape(kv_lora, HEADS * HEAD_PAD).astype(BF16)
    wv = w_ukv[:, :, NOPE:].reshape(kv_lora, HEADS * VDIM).astype(BF16)
    q_gain, k_gain = p["mla_q_gain"][l], p["mla_k_gain"][l]

    ab_re, ab_im, bb_re, bb_im = disc
    scan = []
    for dr in range(2):
        bmat = jnp.concatenate([
            _block_diag(jnp.swapaxes(bb_re[l, dr], 1, 2)),
            _block_diag(jnp.swapaxes(bb_im[l, dr], 1, 2))], axis=1).astype(BF16)
        scan.append(dict(
            bmat=bmat,
            cre=_block_diag(jnp.swapaxes(p["s5_c_re"][l, dr], 1, 2)).astype(BF16),
            cim=_block_diag(jnp.swapaxes(p["s5_c_im"][l, dr], 1, 2)).astype(BF16),
            ar=ab_re[l, dr].reshape(1, -1), ai=ab_im[l, dr].reshape(1, -1),
            wa=_block_diag(p["lru_w_a"][l, dr]).astype(BF16), ba=p["lru_b_a"][l, dr][None],
            wx=_block_diag(p["lru_w_x"][l, dr]).astype(BF16), bx=p["lru_b_x"][l, dr][None],
            lam=p["lru_lambda"][l, dr][None]))

    return dict(
        branch_w=bw, kv_lora=kv_lora, q_lora=q_lora,
        norm_g=p["norm_g"][l][None], w1=w1, w2=w2,
        kv_norm=p["mla_kv_norm"][l][None], q_norm=p["mla_q_norm"][l][None],
        wq=wq, wqp=wqp, wk=wk, wv=wv,
        qgain=_head_pad(q_gain, 0)[None], qgainp=_head_pad(q_gain[NOPE:][perm], NOPE)[None],
        kgain=_head_pad(k_gain, 0)[None], kgainp=_head_pad(k_gain[NOPE:][perm], NOPE)[None],
        conv_w=p["lru_conv_w"][l], conv_b=p["lru_conv_b"][l][None],
        pool_w=_block_diag(p["pool_w"][l]).astype(BF16), pool_b=p["pool_b"][l][None],
        pool_scale=p["pool_scale"][l][None],
        wb=p["w_branch"][l].astype(BF16), wo=p["w_out"][l].astype(BF16),
        wglu=p["s5_w_glu"][l].astype(BF16), s5_d=p["s5_d"][l][None],
        scan=scan)


def kernel(x, c, ctx, c_ctx, w_ada, b_ada, norm_g, w_in, mla_q_norm, mla_kv_norm, mla_w_uq, mla_w_ukv, mla_q_gain, mla_k_gain, s5_a_re, s5_a_im, s5_log_dt, s5_b_re, s5_b_im, s5_c_re, s5_c_im, s5_d, s5_w_glu, lru_conv_w, lru_conv_b, lru_lambda, lru_w_a, lru_b_a, lru_w_x, lru_b_x, pool_w, pool_b, pool_scale, w_branch, w_out):
    p = dict(w_in=w_in, norm_g=norm_g, mla_q_norm=mla_q_norm, mla_kv_norm=mla_kv_norm,
             mla_w_uq=mla_w_uq, mla_w_ukv=mla_w_ukv, mla_q_gain=mla_q_gain, mla_k_gain=mla_k_gain,
             s5_c_re=s5_c_re, s5_c_im=s5_c_im, s5_d=s5_d, s5_w_glu=s5_w_glu,
             lru_conv_w=lru_conv_w, lru_conv_b=lru_conv_b, lru_lambda=lru_lambda,
             lru_w_a=lru_w_a, lru_b_a=lru_b_a, lru_w_x=lru_w_x, lru_b_x=lru_b_x,
             pool_w=pool_w, pool_b=pool_b, pool_scale=pool_scale, w_branch=w_branch, w_out=w_out)
    depth = w_in.shape[0]
    b, l, d = x.shape
    lc = ctx.shape[1]
    tm_l = min(ROW_TILE, l)
    tm_c = min(ROW_TILE, lc)
    tq_l = min(Q_TILE, l)
    tq_c = min(Q_TILE, lc)

    n_rows = -(-(b + 1) // 8) * 8
    cc = jnp.concatenate([c, c_ctx[None], jnp.zeros((n_rows - b - 1, d), F32)], axis=0)
    mod_all = _ada(cc, w_ada, b_ada)
    disc = _s5_discretize(s5_a_re, s5_a_im, s5_log_dt, s5_b_re, s5_b_im)
    cos_l, sin_l = _rope_tables(l, True)
    cos_c, sin_c = _rope_tables(lc, False)

    xc = ctx
    for li in range(depth):
        with_ctx = li < depth - 1
        w = _layer_weights(li, p, disc)
        mod_l = mod_all[li, :b].reshape(b, 3, d)
        mod_c = mod_all[li, b:b + 1].reshape(1, 3, d)

        q_l, k_l, vt_l, u_l, xp_l = _proj(x, mod_l, w, cos_l, sin_l, tm_l)
        q_c, k_c, vt_c, u_c, xp_c = _proj(xc, mod_c, w, cos_c, sin_c, tm_c)
        xl_l, op_l = _seq(xp_l, w)
        xl_c, op_c = _seq(xp_c, w)

        o_mla = _attend(q_l, [k_c, k_l], [vt_c, vt_l], tq_l)

        s_tm = jnp.concatenate([
            jnp.transpose(jnp.concatenate([u_l, xl_l], axis=-1), (1, 0, 2)),
            jnp.transpose(jnp.concatenate([u_c, xl_c], axis=-1), (1, 0, 2))], axis=0)
        y_tm = _scan(s_tm, None, w["scan"][0], l, lc, reverse=False)
        y_tm = _scan(s_tm, y_tm, w["scan"][1], l, lc, reverse=True)
        r_bm = jnp.transpose(y_tm, (1, 0, 2))

        x_new = _merge(x, mod_l, o_mla, r_bm, 0, u_l, op_l, w, tm_l)
        if with_ctx:
            o_mla_c = _attend(q_c, [k_c], [vt_c], tq_c)
            xc = _merge(xc, mod_c, o_mla_c, r_bm, l, u_c, op_c, w, tm_c)
        x = x_new
    return x
```

```python
import functools
import math

import jax
import jax.numpy as jnp
from jax import lax
from jax.experimental import pallas as pl
from jax.experimental.pallas import tpu as pltpu

F32 = jnp.float32
BF16 = jnp.bfloat16

EPS = 1e-6
GRID_W = 64
N_BRANCH = 4
HEADS = 4
NOPE = 64
ROPE = 32
VDIM = 64
QK = NOPE + ROPE
ROPE_THETA = 10000.0
LRU_CONV = 4
LRU_C = 8.0
POOL_WINDOWS = (2, 4, 8, 16)
Q_SCALE = (QK ** -0.5) * math.log2(math.e)

LANE = 128
SUBLANE = 8
HEAD_PAD = LANE
VMEM_LIMIT = 56 * 1024 * 1024

PROJ_STEPS = 128
MERGE_STEPS = 64
Q_TILE = 512
SCAN_CHUNK = 64
SCAN_LANE_SPLITS = 2
HALO = 8


def _dot(a, b):
    return jnp.dot(a, b, preferred_element_type=F32)


def _dot_nt(a, b):
    return lax.dot_general(a, b, (((1,), (1,)), ((), ())), preferred_element_type=F32)


def _rms(x, g):
    return x * lax.rsqrt(jnp.mean(x * x, axis=-1, keepdims=True) + EPS) * g


def _wspec(arr, lead):
    tail = arr.shape[len(lead):]
    idx = tuple(lead) + (0,) * len(tail)
    return pl.BlockSpec((None,) * len(lead) + tail, lambda *_: idx, pipeline_mode=pl.Buffered(1))


def _params(sem):
    return pltpu.CompilerParams(dimension_semantics=sem, vmem_limit_bytes=VMEM_LIMIT)


def _ada_kernel(c_ref, w_ref, b_ref, o_ref):
    c = c_ref[...]
    act = c * jax.nn.sigmoid(c)
    o_ref[0] = jnp.dot(act, w_ref[0], preferred_element_type=F32,
                       precision=lax.Precision.HIGHEST) + b_ref[0]


def _ada(cc, w_ada, b_ada):
    depth, d, d3 = w_ada.shape
    rows = cc.shape[0]
    col = d
    return pl.pallas_call(
        _ada_kernel,
        out_shape=jax.ShapeDtypeStruct((depth, rows, d3), F32),
        grid=(depth, d3 // col),
        in_specs=[pl.BlockSpec((rows, d), lambda l, j: (0, 0)),
                  pl.BlockSpec((1, d, col), lambda l, j: (l, 0, j)),
                  pl.BlockSpec((1, 1, col), lambda l, j: (l, 0, j))],
        out_specs=pl.BlockSpec((1, rows, col), lambda l, j: (l, 0, j)),
        compiler_params=_params(("arbitrary", "arbitrary")),
        name="ada_mod",
    )(cc, w_ada, b_ada.reshape(depth, 1, d3))


def _s5_disc_kernel(are_ref, aim_ref, ldt_ref, bre_ref, bim_ref, abre_ref, abim_ref, bbre_ref, bbim_ref):
    a_re = are_ref[...]
    a_im = aim_ref[...]
    dt = jnp.exp(ldt_ref[...])
    mag = jnp.exp(a_re * dt)
    ab_re = mag * jnp.cos(a_im * dt)
    ab_im = mag * jnp.sin(a_im * dt)
    den = a_re * a_re + a_im * a_im
    f_re = ((ab_re - 1.0) * a_re + ab_im * a_im) / den
    f_im = (ab_im * a_re - (ab_re - 1.0) * a_im) / den
    b_re = bre_ref[...]
    b_im = bim_ref[...]
    abre_ref[...] = ab_re
    abim_ref[...] = ab_im
    bbre_ref[...] = f_re * b_re - f_im * b_im
    bbim_ref[...] = f_re * b_im + f_im * b_re


def _s5_discretize(a_re, a_im, log_dt, b_re, b_im):
    full = b_re.shape
    rows = b_re.size // LANE
    expand = lambda a: jnp.broadcast_to(a[..., None], full).reshape(rows, LANE)
    ins = [expand(a_re), expand(a_im), expand(jnp.broadcast_to(log_dt[..., None], a_re.shape)),
           b_re.reshape(rows, LANE), b_im.reshape(rows, LANE)]
    outs = pl.pallas_call(
        _s5_disc_kernel,
        out_shape=[jax.ShapeDtypeStruct((rows, LANE), F32)] * 4,
        name="s5_discretize",
    )(*ins)
    ab_re, ab_im, bb_re, bb_im = [o.reshape(full) for o in outs]
    return ab_re[..., 0], ab_im[..., 0], bb_re, bb_im


def _proj_kernel(x_ref, mod_ref, ng_ref, w1_ref, kvg_ref, wk_ref, wv_ref, qg_ref, wq_ref, wqp_ref,
                 qgain_ref, qgainp_ref, kgain_ref, kgainp_ref, cos_ref, sin_ref,
                 q_out, k_out, vt_out, u_out, xp_out, *, kv_lora, q_lora, bw):
    nb, tt, d = x_ref.shape
    rows = nb * tt
    mod = mod_ref[...]
    shift, scale = mod[:, 0:1, :], mod[:, 1:2, :]
    h = _rms(x_ref[...], ng_ref[...]) * (1.0 + scale) + shift
    z = _dot(h.reshape(rows, d).astype(BF16), w1_ref[...])

    o = 0
    ckv = z[:, o:o + kv_lora]; o += kv_lora
    kr = z[:, o:o + HEAD_PAD]; o += HEAD_PAD
    kp = z[:, o:o + HEAD_PAD]; o += HEAD_PAD
    cq = z[:, o:o + q_lora]; o += q_lora
    u_out[...] = jnp.transpose(z[:, o:o + bw].reshape(nb, tt, bw), (1, 0, 2)); o += bw
    xp_out[...] = z[:, o:o + 2 * bw].reshape(nb, tt, 2 * bw)

    cos = jnp.concatenate([cos_ref[...]] * nb, axis=0)
    sin = jnp.concatenate([sin_ref[...]] * nb, axis=0)
    inv_qk = 1.0 / QK

    cqn = _rms(cq, qg_ref[...]).astype(BF16)
    qraw = _dot(cqn, wq_ref[...])
    qprt = _dot(cqn, wqp_ref[...])
    qgain, qgainp = qgain_ref[...], qgainp_ref[...]
    for hh in range(HEADS):
        sl = slice(hh * HEAD_PAD, (hh + 1) * HEAD_PAD)
        qh = qraw[:, sl]
        rs = lax.rsqrt(jnp.sum(qh * qh, axis=-1, keepdims=True) * inv_qk + EPS)
        qn = qh * rs * qgain
        qpn = qprt[:, sl] * rs * qgainp
        q_out[:, hh] = ((qn * cos + qpn * sin) * Q_SCALE).astype(BF16).reshape(nb, tt, HEAD_PAD)

    ckvn = _rms(ckv, kvg_ref[...]).astype(BF16)
    knope = _dot(ckvn, wk_ref[...])
    v = _dot(ckvn, wv_ref[...])
    kgain, kgainp = kgain_ref[...], kgainp_ref[...]
    for hh in range(HEADS):
        sl = slice(hh * HEAD_PAD, (hh + 1) * HEAD_PAD)
        kh = knope[:, sl] + kr
        rs = lax.rsqrt(jnp.sum(kh * kh, axis=-1, keepdims=True) * inv_qk + EPS)
        kn = kh * rs * kgain
        kpn = kp * rs * kgainp
        k_out[:, hh] = (kn * cos + kpn * sin).astype(BF16).reshape(nb, tt, HEAD_PAD)
    for bi in range(nb):
        vt_out[bi] = v[bi * tt:(bi + 1) * tt].T.astype(BF16).reshape(HEADS, VDIM, tt)


def _proj(xx, mod3, w, li, cos_t, sin_t, n_lat):
    b, t, d = xx.shape
    nb, tt = SUBLANE, PROJ_STEPS
    bw = d // N_BRANCH
    kv_lora, q_lora = w["kv_norm"].shape[-1], w["q_norm"].shape[-1]
    ctx_group = mod3.shape[1] - 1
    lat_tiles = n_lat // tt
    mod_map = lambda i, j: (li, jnp.where(j < lat_tiles, i, ctx_group), 0, 0, 0)
    kern = functools.partial(_proj_kernel, kv_lora=kv_lora, q_lora=q_lora, bw=bw)
    names = ["norm_g", "w1", "kv_norm", "wk", "wv", "q_norm", "wq", "wqp", "qgain", "qgainp", "kgain", "kgainp"]
    return pl.pallas_call(
        kern,
        out_shape=[jax.ShapeDtypeStruct((b, HEADS, t, HEAD_PAD), BF16),
                   jax.ShapeDtypeStruct((b, HEADS, t, HEAD_PAD), BF16),
                   jax.ShapeDtypeStruct((b, HEADS, VDIM, t), BF16),
                   jax.ShapeDtypeStruct((t, b, bw), F32),
                   jax.ShapeDtypeStruct((b, t, 2 * bw), F32)],
        grid=(b // nb, t // tt),
        in_specs=[pl.BlockSpec((nb, tt, d), lambda i, j: (i, j, 0)),
                  pl.BlockSpec((None, None, nb, 3, d), mod_map)]
                 + [_wspec(w[n], (li,)) for n in names]
                 + [pl.BlockSpec((tt, HEAD_PAD), lambda i, j: (j, 0)),
                    pl.BlockSpec((tt, HEAD_PAD), lambda i, j: (j, 0))],
        out_specs=[pl.BlockSpec((nb, HEADS, tt, HEAD_PAD), lambda i, j: (i, 0, j, 0)),
                   pl.BlockSpec((nb, HEADS, tt, HEAD_PAD), lambda i, j: (i, 0, j, 0)),
                   pl.BlockSpec((nb, HEADS, VDIM, tt), lambda i, j: (i, 0, 0, j)),
                   pl.BlockSpec((tt, nb, bw), lambda i, j: (j, i, 0)),
                   pl.BlockSpec((nb, tt, 2 * bw), lambda i, j: (i, j, 0))],
        compiler_params=_params(("arbitrary", "arbitrary")),
        name="proj_qkv",
    )(xx, mod3, *[w[n] for n in names], cos_t, sin_t)


def _seq_kernel(xp_ref, cw_ref, cb_ref, pw_ref, pb_ref, ps_ref, xl_out, op_out, pad_ref, *, segs, bw):
    zeros = jnp.zeros((HALO, 2 * bw), F32)
    cw = cw_ref[...]
    left = LRU_CONV // 2
    group_w = bw // len(POOL_WINDOWS)
    for lo, t in segs:
        pad_ref[0:HALO, :] = zeros
        pad_ref[HALO + t:2 * HALO + t, :] = zeros
        pad_ref[HALO:HALO + t, :] = xp_ref[0, lo:lo + t, :]

        def shifted(s, c0, c1, t=t):
            return pad_ref[HALO + s:HALO + s + t, c0:c1]

        acc = cb_ref[...] + shifted(-left, 0, bw) * cw[0:1]
        for k in range(1, LRU_CONV):
            acc = acc + shifted(k - left, 0, bw) * cw[k:k + 1]
        xl_out[0, lo:lo + t, :] = acc

        xc = shifted(0, bw, 2 * bw)
        lane = lax.broadcasted_iota(jnp.int32, (t, bw), 1)
        row = lax.broadcasted_iota(jnp.int32, (t, bw), 0)
        s = shifted(-1, bw, 2 * bw) + xc
        sel = s
        half = jnp.ones((t, bw), jnp.int32)
        for gi in range(1, len(POOL_WINDOWS)):
            hw_prev = POOL_WINDOWS[gi - 1] // 2
            hw = POOL_WINDOWS[gi] // 2
            for k in range(hw_prev, hw):
                s = s + shifted(-(k + 1), bw, 2 * bw) + shifted(k, bw, 2 * bw)
            in_group = lane >= gi * group_w
            sel = jnp.where(in_group, s, sel)
            half = jnp.where(in_group, hw, half)
        cnt = (jnp.minimum(row + half, t) - jnp.maximum(row - half, 0)).astype(F32)
        p = sel / cnt - xc
        y = _dot(p.astype(BF16), pw_ref[...]) + pb_ref[...]
        op_out[0, lo:lo + t, :] = y * ps_ref[...]


def _seq(xp, w, li, segs):
    b, t, w2 = xp.shape
    bw = w2 // 2
    kern = functools.partial(_seq_kernel, segs=segs, bw=bw)
    names = ["conv_w", "conv_b", "pool_w", "pool_b", "pool_scale"]
    max_t = max(n for _, n in segs)
    return pl.pallas_call(
        kern,
        out_shape=[jax.ShapeDtypeStruct((b, t, bw), F32), jax.ShapeDtypeStruct((b, t, bw), F32)],
        grid=(b,),
        in_specs=[pl.BlockSpec((1, t, w2), lambda i: (i, 0, 0))] + [_wspec(w[n], (li,)) for n in names],
        out_specs=[pl.BlockSpec((1, t, bw), lambda i: (i, 0, 0)),
                   pl.BlockSpec((1, t, bw), lambda i: (i, 0, 0))],
        scratch_shapes=[pltpu.VMEM((max_t + 2 * HALO, w2), F32)],
        compiler_params=_params(("arbitrary",)),
        name="conv_pool",
    )(xp, *[w[n] for n in names])


def _attn_kernel(q_ref, k_ref, vt_ref, o_ref, s_buf):
    def scores(hh, slot):
        sc = _dot_nt(k_ref[0, hh], q_ref[0, hh])
        s_buf[slot] = sc
        return jnp.max(sc, axis=0, keepdims=True)

    def weighted_values(hh, slot, m):
        e = jnp.exp2(s_buf[slot] - m)
        den = jnp.sum(e, axis=0, keepdims=True)
        return _dot(vt_ref[0, hh], e.astype(BF16)) / den

    outs = []
    m_cur = scores(0, 0)
    for hh in range(HEADS):
        m_next = scores(hh + 1, (hh + 1) % 2) if hh + 1 < HEADS else None
        outs.append(weighted_values(hh, hh % 2, m_cur))
        m_cur = m_next
    o_ref[0] = jnp.concatenate(outs, axis=0).T


def _attend(q, k, vt, n_q, q_off, n_k, k_off, tq):
    b = q.shape[0]
    hv = HEADS * VDIM
    assert n_q % tq == 0 and q_off % tq == 0 and k_off % n_k == 0
    qo, ko = q_off // tq, k_off // n_k
    return pl.pallas_call(
        _attn_kernel,
        out_shape=jax.ShapeDtypeStruct((b, n_q, hv), F32),
        grid=(b, n_q // tq),
        in_specs=[pl.BlockSpec((1, HEADS, tq, HEAD_PAD), lambda i, j: (i, 0, j + qo, 0)),
                  pl.BlockSpec((1, HEADS, n_k, HEAD_PAD), lambda i, j: (i, 0, ko, 0)),
                  pl.BlockSpec((1, HEADS, VDIM, n_k), lambda i, j: (i, 0, 0, ko))],
        out_specs=pl.BlockSpec((1, tq, hv), lambda i, j: (i, j, 0)),
        scratch_shapes=[pltpu.VMEM((2, n_k, tq), F32)],
        compiler_params=_params(("arbitrary", "arbitrary")),
        name="attention",
    )(q, k, vt)


def _scan_kernel(uf_ref, ub_ref, xf_ref, xb_ref, bmat_ref, cre_ref, cim_ref, ar_ref, ai_ref,
                 wa_ref, ba_ref, wx_ref, bx_ref, lam_ref, yf_ref, yb_ref,
                 hbuf, abuf, bbuf, lbuf, hr_s, hi_s, hl_s, *, tc, bsz, bw, ns):
    r = tc * bsz

    @pl.when(pl.program_id(0) == 0)
    def _():
        hr_s[...] = jnp.zeros_like(hr_s)
        hi_s[...] = jnp.zeros_like(hi_s)
        hl_s[...] = jnp.zeros_like(hl_s)

    for d, (u_ref, x_ref) in enumerate(((uf_ref, xf_ref), (ub_ref, xb_ref))):
        u = u_ref[...].reshape(r, bw)
        xl = jnp.transpose(x_ref[...], (1, 0, 2)).reshape(r, bw)
        hbuf[d] = _dot(u.astype(BF16), bmat_ref[d])
        xb = xl.astype(BF16)
        rg = jax.nn.sigmoid(_dot(xb, wa_ref[d]) + ba_ref[d])
        ig = jax.nn.sigmoid(_dot(xb, wx_ref[d]) + bx_ref[d])
        nl = -lam_ref[d]
        softplus = jnp.maximum(nl, 0.0) + jnp.log1p(jnp.exp(-jnp.abs(nl)))
        log_a = (-LRU_C) * rg * softplus
        abuf[d] = jnp.exp(log_a)
        th = jnp.tanh(log_a)
        bbuf[d] = jnp.sqrt(-2.0 * th / (1.0 - th)) * (ig * xl)

    def rows(t, d):
        te = (tc - 1 - t) if d else t
        return pl.ds(pl.multiple_of(te * bsz, bsz), bsz)

    def lru_body(t, carry):
        out = []
        for d in range(2):
            rw = rows(t, d)
            hl = abuf[d, rw, :] * carry[d] + bbuf[d, rw, :]
            lbuf[d, rw, :] = hl
            out.append(hl)
        return tuple(out)

    hl0, hl1 = lax.fori_loop(0, tc, lru_body, (hl_s[0], hl_s[1]))
    hl_s[0] = hl0
    hl_s[1] = hl1

    w = ns // SCAN_LANE_SPLITS
    for j in range(SCAN_LANE_SPLITS):
        re_sl = slice(j * w, (j + 1) * w)
        im_sl = slice(ns + j * w, ns + (j + 1) * w)
        coef = [(jnp.broadcast_to(ar_ref[d, :, re_sl], (bsz, w)), jnp.broadcast_to(ai_ref[d, :, re_sl], (bsz, w)))
                for d in range(2)]

        def s5_body(t, carry, re_sl=re_sl, im_sl=im_sl, coef=coef):
            out = []
            for d in range(2):
                hr, hi = carry[2 * d], carry[2 * d + 1]
                ar, ai = coef[d]
                rw = rows(t, d)
                nhr = ar * hr - ai * hi + hbuf[d, rw, re_sl]
                nhi = ar * hi + ai * hr + hbuf[d, rw, im_sl]
                hbuf[d, rw, re_sl] = nhr
                hbuf[d, rw, im_sl] = nhi
                out += [nhr, nhi]
            return tuple(out)

        fin = lax.fori_loop(0, tc, s5_body, (hr_s[0, :, re_sl], hi_s[0, :, re_sl], hr_s[1, :, re_sl], hi_s[1, :, re_sl]))
        for d in range(2):
            hr_s[d, :, re_sl] = fin[2 * d]
            hi_s[d, :, re_sl] = fin[2 * d + 1]

    for d, y_ref in enumerate((yf_ref, yb_ref)):
        y = (_dot(hbuf[d, :, 0:ns].astype(BF16), cre_ref[d])
             - _dot(hbuf[d, :, ns:2 * ns].astype(BF16), cim_ref[d]))
        y_ref[:, :, 0:bw] = y.reshape(tc, bsz, bw)
        y_ref[:, :, bw:2 * bw] = lbuf[d].reshape(tc, bsz, bw)


def _scan(u_tm, xl, w, li, n_lat):
    t, bsz, bw = u_tm.shape
    tc = SCAN_CHUNK
    nc = t // tc
    ncl = n_lat // tc
    ncc = nc - ncl
    ns = w["ar"].shape[-1]
    fwd = lambda s: jnp.where(s < ncc, ncl + s, s - ncc)
    bwd = lambda s: nc - 1 - s
    names = ["bmat", "cre", "cim", "ar", "ai", "wa", "ba", "wx", "bx", "lam"]
    r = tc * bsz
    kern = functools.partial(_scan_kernel, tc=tc, bsz=bsz, bw=bw, ns=ns)
    return pl.pallas_call(
        kern,
        out_shape=[jax.ShapeDtypeStruct((t, bsz, 2 * bw), F32)] * 2,
        grid=(nc,),
        in_specs=[pl.BlockSpec((tc, bsz, bw), lambda s: (fwd(s), 0, 0)),
                  pl.BlockSpec((tc, bsz, bw), lambda s: (bwd(s), 0, 0)),
                  pl.BlockSpec((bsz, tc, bw), lambda s: (0, fwd(s), 0)),
                  pl.BlockSpec((bsz, tc, bw), lambda s: (0, bwd(s), 0))]
                 + [_wspec(w[n], (li,)) for n in names],
        out_specs=[pl.BlockSpec((tc, bsz, 2 * bw), lambda s: (fwd(s), 0, 0)),
                   pl.BlockSpec((tc, bsz, 2 * bw), lambda s: (bwd(s), 0, 0))],
        scratch_shapes=[pltpu.VMEM((2, r, 2 * ns), F32), pltpu.VMEM((2, r, bw), F32), pltpu.VMEM((2, r, bw), F32),
                        pltpu.VMEM((2, r, bw), F32), pltpu.VMEM((2, bsz, ns), F32), pltpu.VMEM((2, bsz, ns), F32),
                        pltpu.VMEM((2, bsz, bw), F32)],
        compiler_params=_params(("arbitrary",)),
        name="scan_bidir",
    )(u_tm, u_tm, xl, xl, *[w[n] for n in names])


def _merge_kernel(*refs, bw, lat_tiles, with_ctx):
    if with_ctx:
        (x_ref, mod_ref, ng_ref, om_ref, omc_ref, yf_ref, yb_ref, u_ref, op_ref,
         w2_ref, wb_ref, wo_ref, wglu_ref, d_ref, out_ref) = refs
    else:
        (x_ref, mod_ref, ng_ref, om_ref, yf_ref, yb_ref, u_ref, op_ref,
         w2_ref, wb_ref, wo_ref, wglu_ref, d_ref, out_ref) = refs
    nb, tt, d = x_ref.shape
    rows = nb * tt
    x = x_ref[...]
    mod = mod_ref[...]
    shift, scale, gate = mod[:, 0:1, :], mod[:, 1:2, :], mod[:, 2:3, :]
    h = (_rms(x, ng_ref[...]) * (1.0 + scale) + shift).reshape(rows, d).astype(BF16)

    rr = jnp.transpose(yf_ref[...] + yb_ref[...], (1, 0, 2)).reshape(rows, 2 * bw)
    u = jnp.transpose(u_ref[...], (1, 0, 2)).reshape(rows, bw)
    g = jax.nn.gelu(rr[:, 0:bw] + d_ref[...] * u, approximate=True)
    o_s5 = g * jax.nn.sigmoid(_dot(g.astype(BF16), wglu_ref[...]))
    o_mla = om_ref[...]
    if with_ctx:
        o_mla = jnp.where(pl.program_id(1) >= lat_tiles, omc_ref[...], o_mla)
    branches = (o_mla.reshape(rows, bw), o_s5, rr[:, bw:2 * bw], op_ref[...].reshape(rows, bw))

    y = None
    for n in range(N_BRANCH):
        gp = _dot(h, w2_ref[:, n * bw:(n + 1) * bw])
        off = N_BRANCH * bw + n * d
        ml = _dot(h, w2_ref[:, off:off + d])
        tn = _dot((branches[n] * (gp * jax.nn.sigmoid(gp))).astype(BF16), wb_ref[n])
        term = jax.nn.sigmoid(ml) * tn
        y = term if y is None else y + term
    out_ref[...] = x + gate * _dot(y.astype(BF16), wo_ref[...]).reshape(nb, tt, d)


def _merge(xx, mod3, o_lat, o_ctx, yf, yb, u_tm, o_pool, w, li, n_lat, n_out):
    b, t, d = xx.shape
    nb, tt = SUBLANE, MERGE_STEPS
    bw = d // N_BRANCH
    with_ctx = o_ctx is not None
    lat_tiles = n_lat // tt
    ctx_group = mod3.shape[1] - 1
    mod_map = lambda i, j: (li, jnp.where(j < lat_tiles, i, ctx_group), 0, 0, 0)
    bm = lambda width: pl.BlockSpec((nb, tt, width), lambda i, j: (i, j, 0))
    tm = lambda width: pl.BlockSpec((tt, nb, width), lambda i, j: (j, i, 0))
    names = ["w2", "wb", "wo", "wglu", "s5_d"]
    in_specs = [bm(d), pl.BlockSpec((None, None, nb, 3, d), mod_map), _wspec(w["norm_g"], (li,)),
                pl.BlockSpec((nb, tt, bw), lambda i, j: (i, jnp.minimum(j, lat_tiles - 1), 0))]
    args = [xx, mod3, w["norm_g"], o_lat]
    if with_ctx:
        ctx_tiles = o_ctx.shape[1] // tt
        in_specs.append(pl.BlockSpec((nb, tt, bw), lambda i, j: (i, jnp.clip(j - lat_tiles, 0, ctx_tiles - 1), 0)))
        args.append(o_ctx)
    in_specs += [tm(2 * bw), tm(2 * bw), tm(bw), bm(bw)] + [_wspec(w[n], (li,)) for n in names]
    args += [yf, yb, u_tm, o_pool] + [w[n] for n in names]
    kern = functools.partial(_merge_kernel, bw=bw, lat_tiles=lat_tiles, with_ctx=with_ctx)
    return pl.pallas_call(
        kern,
        out_shape=jax.ShapeDtypeStruct((b, n_out, d), F32),
        grid=(b // nb, n_out // tt),
        in_specs=in_specs,
        out_specs=bm(d),
        compiler_params=_params(("arbitrary", "arbitrary")),
        name="merge_out",
    )(*args)


def _block_diag(blocks):
    n, a, b = blocks.shape[-3:]
    eye = jnp.eye(n, dtype=blocks.dtype)
    return jnp.einsum("...nab,nm->...namb", blocks, eye).reshape(blocks.shape[:-3] + (n * a, n * b))


def _rope_perm():
    half = ROPE // 2
    nf = half // 2
    base = jnp.concatenate([jnp.arange(nf, half), jnp.arange(0, nf)])
    return jnp.concatenate([base, base + half])


def _head_pad(a, lo):
    n = a.shape[-1]
    pad = [(0, 0)] * (a.ndim - 1) + [(lo, HEAD_PAD - lo - n)]
    return jnp.pad(a, pad)


def _rope_tables(n_lat, n_ctx):
    rows_n = n_lat // GRID_W
    row = jnp.repeat(jnp.arange(rows_n, dtype=jnp.int32), GRID_W).astype(F32)
    col = jnp.tile(jnp.arange(GRID_W, dtype=jnp.int32), rows_n).astype(F32)
    nf = ROPE // 4
    inv = ROPE_THETA ** (-jnp.arange(nf, dtype=F32) / nf)
    ang_r = row[:, None] * inv
    ang_c = col[:, None] * inv
    cr, sr, cc, sc = jnp.cos(ang_r), jnp.sin(ang_r), jnp.cos(ang_c), jnp.sin(ang_c)
    ones = jnp.ones((n_lat, NOPE), F32)
    zeros = jnp.zeros((n_lat, NOPE), F32)
    tail = HEAD_PAD - QK
    cos = jnp.concatenate([ones, cr, cr, cc, cc, jnp.ones((n_lat, tail), F32)], -1)
    sin = jnp.concatenate([zeros, -sr, sr, -sc, sc, jnp.zeros((n_lat, tail), F32)], -1)
    cos = jnp.concatenate([cos, jnp.ones((n_ctx, HEAD_PAD), F32)], 0)
    sin = jnp.concatenate([sin, jnp.zeros((n_ctx, HEAD_PAD), F32)], 0)
    return cos, sin


def _prep_weights(p, disc):
    w_in = p["w_in"]
    depth, d, _ = w_in.shape
    bw = d // N_BRANCH
    kv_lora = p["mla_kv_norm"].shape[-1]
    q_lora = p["mla_q_norm"].shape[-1]
    perm = _rope_perm()
    off_krope = kv_lora
    off_s5 = off_krope + ROPE
    off_lru = off_s5 + bw
    off_cq = off_lru + bw
    off_pool = off_cq + q_lora
    off_gate = off_pool + bw
    w_krope = w_in[:, :, off_krope:off_s5]
    w1 = jnp.concatenate([
        w_in[:, :, :kv_lora],
        _head_pad(w_krope, NOPE),
        _head_pad(w_krope[:, :, perm], NOPE),
        w_in[:, :, off_cq:off_pool],
        w_in[:, :, off_s5:off_lru],
        w_in[:, :, off_lru:off_cq],
        w_in[:, :, off_pool:off_gate]], axis=2).astype(BF16)
    w2 = w_in[:, :, off_gate:].astype(BF16)

    w_uq = p["mla_w_uq"].reshape(depth, q_lora, HEADS, QK)
    wq = _head_pad(w_uq, 0).reshape(depth, q_lora, HEADS * HEAD_PAD).astype(BF16)
    wqp = _head_pad(w_uq[..., NOPE:][..., perm], NOPE).reshape(depth, q_lora, HEADS * HEAD_PAD).astype(BF16)
    w_ukv = p["mla_w_ukv"].reshape(depth, kv_lora, HEADS, NOPE + VDIM)
    wk = _head_pad(w_ukv[..., :NOPE], 0).reshape(depth, kv_lora, HEADS * HEAD_PAD).astype(BF16)
    wv = w_ukv[..., NOPE:].reshape(depth, kv_lora, HEADS * VDIM).astype(BF16)
    q_gain, k_gain = p["mla_q_gain"], p["mla_k_gain"]
    row = lambda a: a[..., None, :]

    ab_re, ab_im, bb_re, bb_im = disc
    sw = lambda a: jnp.swapaxes(a, -1, -2)
    bmat = jnp.concatenate([_block_diag(sw(bb_re)), _block_diag(sw(bb_im))], axis=-1).astype(BF16)
    return dict(
        norm_g=row(p["norm_g"]), w1=w1, w2=w2,
        kv_norm=row(p["mla_kv_norm"]), q_norm=row(p["mla_q_norm"]),
        wq=wq, wqp=wqp, wk=wk, wv=wv,
        qgain=row(_head_pad(q_gain, 0)), qgainp=row(_head_pad(q_gain[:, NOPE:][:, perm], NOPE)),
        kgain=row(_head_pad(k_gain, 0)), kgainp=row(_head_pad(k_gain[:, NOPE:][:, perm], NOPE)),
        conv_w=p["lru_conv_w"], conv_b=row(p["lru_conv_b"]),
        pool_w=_block_diag(p["pool_w"]).astype(BF16), pool_b=row(p["pool_b"]),
        pool_scale=row(p["pool_scale"]),
        wb=p["w_branch"].astype(BF16), wo=p["w_out"].astype(BF16),
        wglu=p["s5_w_glu"].astype(BF16), s5_d=row(p["s5_d"]),
        bmat=bmat,
        cre=_block_diag(sw(p["s5_c_re"])).astype(BF16), cim=_block_diag(sw(p["s5_c_im"])).astype(BF16),
        ar=row(ab_re.reshape(ab_re.shape[:2] + (-1,))), ai=row(ab_im.reshape(ab_im.shape[:2] + (-1,))),
        wa=_block_diag(p["lru_w_a"]).astype(BF16), ba=row(p["lru_b_a"]),
        wx=_block_diag(p["lru_w_x"]).astype(BF16), bx=row(p["lru_b_x"]),
        lam=row(p["lru_lambda"]))


def kernel(x, c, ctx, c_ctx, w_ada, b_ada, norm_g, w_in, mla_q_norm, mla_kv_norm, mla_w_uq, mla_w_ukv, mla_q_gain, mla_k_gain, s5_a_re, s5_a_im, s5_log_dt, s5_b_re, s5_b_im, s5_c_re, s5_c_im, s5_d, s5_w_glu, lru_conv_w, lru_conv_b, lru_lambda, lru_w_a, lru_b_a, lru_w_x, lru_b_x, pool_w, pool_b, pool_scale, w_branch, w_out):
    p = dict(w_in=w_in, norm_g=norm_g, mla_q_norm=mla_q_norm, mla_kv_norm=mla_kv_norm,
             mla_w_uq=mla_w_uq, mla_w_ukv=mla_w_ukv, mla_q_gain=mla_q_gain, mla_k_gain=mla_k_gain,
             s5_c_re=s5_c_re, s5_c_im=s5_c_im, s5_d=s5_d, s5_w_glu=s5_w_glu,
             lru_conv_w=lru_conv_w, lru_conv_b=lru_conv_b, lru_lambda=lru_lambda,
             lru_w_a=lru_w_a, lru_b_a=lru_b_a, lru_w_x=lru_w_x, lru_b_x=lru_b_x,
             pool_w=pool_w, pool_b=pool_b, pool_scale=pool_scale, w_branch=w_branch, w_out=w_out)
    depth = w_in.shape[0]
    b, l, d = x.shape
    lc = ctx.shape[1]
    t = l + lc
    assert b % SUBLANE == 0 and l % PROJ_STEPS == 0 and lc % PROJ_STEPS == 0 and l % lc == 0

    n_rows = -(-(b + 1) // SUBLANE) * SUBLANE
    cc = jnp.concatenate([c, c_ctx[None], jnp.zeros((n_rows - b - 1, d), F32)], axis=0)
    mod_all = _ada(cc, w_ada, b_ada)
    mod3 = jnp.concatenate([
        mod_all[:, :b].reshape(depth, b // SUBLANE, SUBLANE, 3, d),
        jnp.broadcast_to(mod_all[:, b:b + 1].reshape(depth, 1, 1, 3, d), (depth, 1, SUBLANE, 3, d))], axis=1)
    disc = _s5_discretize(s5_a_re, s5_a_im, s5_log_dt, s5_b_re, s5_b_im)
    w = _prep_weights(p, disc)
    cos_t, sin_t = _rope_tables(l, lc)
    tq_l, tq_c = min(Q_TILE, l), min(Q_TILE, lc)

    xx = jnp.concatenate([x, ctx], axis=1)
    for li in range(depth):
        with_ctx = li < depth - 1
        q, k, vt, u_tm, xp = _proj(xx, mod3, w, li, cos_t, sin_t, l)
        xl, o_pool = _seq(xp, w, li, ((0, l), (l, lc)))
        o_lat = _attend(q, k, vt, l, 0, t, 0, tq_l)
        o_ctx = _attend(q, k, vt, lc, l, lc, l, tq_c) if with_ctx else None
        yf, yb = _scan(u_tm, xl, w, li, l)
        xx = _merge(xx, mod3, o_lat, o_ctx, yf, yb, u_tm, o_pool, w, li, l, t if with_ctx else l)
    return xx
```

```python
import functools
import math

import jax
import jax.numpy as jnp
from jax import lax
from jax.experimental import pallas as pl
from jax.experimental.pallas import tpu as pltpu

F32 = jnp.float32
BF16 = jnp.bfloat16

EPS = 1e-6
GRID_W = 64
N_BRANCH = 4
HEADS = 4
NOPE = 64
ROPE = 32
VDIM = 64
QK = NOPE + ROPE
ROPE_THETA = 10000.0
LRU_CONV = 4
LRU_C = 8.0
POOL_WINDOWS = (2, 4, 8, 16)
Q_SCALE = (QK ** -0.5) * math.log2(math.e)

LANE = 128
SUBLANE = 8
HEAD_PAD = LANE
VMEM_LIMIT = 56 * 1024 * 1024

PROJ_STEPS = 128
MERGE_STEPS = 64
Q_TILE = 512
SCAN_CHUNK = 64
SCAN_SUB = 8
SCAN_LANE_SPLITS = 2
HALO = 8


def _dot(a, b):
    return jnp.dot(a, b, preferred_element_type=F32)


def _dot_nt(a, b):
    return lax.dot_general(a, b, (((1,), (1,)), ((), ())), preferred_element_type=F32)


def _rms(x, g):
    return x * lax.rsqrt(jnp.mean(x * x, axis=-1, keepdims=True) + EPS) * g


def _wspec(arr, lead):
    tail = arr.shape[len(lead):]
    idx = tuple(lead) + (0,) * len(tail)
    return pl.BlockSpec((None,) * len(lead) + tail, lambda *_: idx, pipeline_mode=pl.Buffered(1))


def _token_specs(toks, nb, tt, lat_tiles):
    if len(toks) == 1:
        return [pl.BlockSpec((nb, tt, toks[0].shape[2]), lambda i, j: (i, j, 0))]
    lat, ctx = toks
    ctx_tiles = ctx.shape[1] // tt
    return [pl.BlockSpec((nb, tt, lat.shape[2]), lambda i, j: (i, jnp.minimum(j, lat_tiles - 1), 0)),
            pl.BlockSpec((nb, tt, ctx.shape[2]), lambda i, j: (i, jnp.clip(j - lat_tiles, 0, ctx_tiles - 1), 0))]


def _read_tokens(tok_refs, lat_tiles):
    if len(tok_refs) == 1:
        return tok_refs[0][...]
    return jnp.where(pl.program_id(1) >= lat_tiles, tok_refs[1][...], tok_refs[0][...])


def _params(sem):
    return pltpu.CompilerParams(dimension_semantics=sem, vmem_limit_bytes=VMEM_LIMIT)


def _ada_kernel(c_ref, w_ref, b_ref, o_ref):
    c = c_ref[...]
    act = c * jax.nn.sigmoid(c)
    o_ref[0] = jnp.dot(act, w_ref[0], preferred_element_type=F32,
                       precision=lax.Precision.HIGHEST) + b_ref[0]


def _ada(cc, w_ada, b_ada):
    depth, d, d3 = w_ada.shape
    rows = cc.shape[0]
    col = d
    return pl.pallas_call(
        _ada_kernel,
        out_shape=jax.ShapeDtypeStruct((depth, rows, d3), F32),
        grid=(depth, d3 // col),
        in_specs=[pl.BlockSpec((rows, d), lambda l, j: (0, 0)),
                  pl.BlockSpec((1, d, col), lambda l, j: (l, 0, j)),
                  pl.BlockSpec((1, 1, col), lambda l, j: (l, 0, j))],
        out_specs=pl.BlockSpec((1, rows, col), lambda l, j: (l, 0, j)),
        compiler_params=_params(("arbitrary", "arbitrary")),
        name="ada_mod",
    )(cc, w_ada, b_ada.reshape(depth, 1, d3))


def _s5_disc_kernel(are_ref, aim_ref, ldt_ref, bre_ref, bim_ref, abre_ref, abim_ref, bbre_ref, bbim_ref):
    a_re = are_ref[...]
    a_im = aim_ref[...]
    dt = jnp.exp(ldt_ref[...])
    mag = jnp.exp(a_re * dt)
    ab_re = mag * jnp.cos(a_im * dt)
    ab_im = mag * jnp.sin(a_im * dt)
    den = a_re * a_re + a_im * a_im
    f_re = ((ab_re - 1.0) * a_re + ab_im * a_im) / den
    f_im = (ab_im * a_re - (ab_re - 1.0) * a_im) / den
    b_re = bre_ref[...]
    b_im = bim_ref[...]
    abre_ref[...] = ab_re
    abim_ref[...] = ab_im
    bbre_ref[...] = f_re * b_re - f_im * b_im
    bbim_ref[...] = f_re * b_im + f_im * b_re


def _s5_discretize(a_re, a_im, log_dt, b_re, b_im):
    full = b_re.shape
    rows = b_re.size // LANE
    expand = lambda a: jnp.broadcast_to(a[..., None], full).reshape(rows, LANE)
    ins = [expand(a_re), expand(a_im), expand(jnp.broadcast_to(log_dt[..., None], a_re.shape)),
           b_re.reshape(rows, LANE), b_im.reshape(rows, LANE)]
    outs = pl.pallas_call(
        _s5_disc_kernel,
        out_shape=[jax.ShapeDtypeStruct((rows, LANE), F32)] * 4,
        name="s5_discretize",
    )(*ins)
    ab_re, ab_im, bb_re, bb_im = [o.reshape(full) for o in outs]
    return ab_re[..., 0], ab_im[..., 0], bb_re, bb_im


def _proj_kernel(*refs, kv_lora, q_lora, bw, n_tok, lat_tiles):
    tok_refs = refs[:n_tok]
    (mod_ref, ng_ref, w1_ref, kvg_ref, wk_ref, wv_ref, qg_ref, wq_ref, wqp_ref,
     qgain_ref, qgainp_ref, kgain_ref, kgainp_ref, cos_ref, sin_ref,
     q_out, k_out, vt_out, u_out, xp_out) = refs[n_tok:]
    nb, tt, d = tok_refs[0].shape
    rows = nb * tt
    mod = mod_ref[...]
    shift, scale = mod[:, 0:1, :], mod[:, 1:2, :]
    h = _rms(_read_tokens(tok_refs, lat_tiles), ng_ref[...]) * (1.0 + scale) + shift
    z = _dot(h.reshape(rows, d).astype(BF16), w1_ref[...])

    o = 0
    ckv = z[:, o:o + kv_lora]; o += kv_lora
    kr = z[:, o:o + HEAD_PAD]; o += HEAD_PAD
    kp = z[:, o:o + HEAD_PAD]; o += HEAD_PAD
    cq = z[:, o:o + q_lora]; o += q_lora
    u_out[...] = jnp.transpose(z[:, o:o + bw].reshape(nb, tt, bw), (1, 0, 2)); o += bw
    xp_out[...] = z[:, o:o + 2 * bw].reshape(nb, tt, 2 * bw)

    cos = jnp.concatenate([cos_ref[...]] * nb, axis=0)
    sin = jnp.concatenate([sin_ref[...]] * nb, axis=0)
    inv_qk = 1.0 / QK

    cqn = _rms(cq, qg_ref[...]).astype(BF16)
    qraw = _dot(cqn, wq_ref[...])
    qprt = _dot(cqn, wqp_ref[...])
    qgain, qgainp = qgain_ref[...], qgainp_ref[...]
    for hh in range(HEADS):
        sl = slice(hh * HEAD_PAD, (hh + 1) * HEAD_PAD)
        qh = qraw[:, sl]
        rs = lax.rsqrt(jnp.sum(qh * qh, axis=-1, keepdims=True) * inv_qk + EPS)
        qn = qh * rs * qgain
        qpn = qprt[:, sl] * rs * qgainp
        q_out[:, hh] = ((qn * cos + qpn * sin) * Q_SCALE).astype(BF16).reshape(nb, tt, HEAD_PAD)

    ckvn = _rms(ckv, kvg_ref[...]).astype(BF16)
    knope = _dot(ckvn, wk_ref[...])
    v = _dot(ckvn, wv_ref[...])
    kgain, kgainp = kgain_ref[...], kgainp_ref[...]
    for hh in range(HEADS):
        sl = slice(hh * HEAD_PAD, (hh + 1) * HEAD_PAD)
        kh = knope[:, sl] + kr
        rs = lax.rsqrt(jnp.sum(kh * kh, axis=-1, keepdims=True) * inv_qk + EPS)
        kn = kh * rs * kgain
        kpn = kp * rs * kgainp
        k_out[:, hh] = (kn * cos + kpn * sin).astype(BF16).reshape(nb, tt, HEAD_PAD)
    for bi in range(nb):
        vt_out[bi] = v[bi * tt:(bi + 1) * tt].T.astype(BF16).reshape(HEADS, VDIM, tt)


def _proj(toks, mod3, w, li, cos_t, sin_t, n_lat):
    b, _, d = toks[0].shape
    t = sum(a.shape[1] for a in toks)
    nb, tt = SUBLANE, PROJ_STEPS
    bw = d // N_BRANCH
    kv_lora, q_lora = w["kv_norm"].shape[-1], w["q_norm"].shape[-1]
    ctx_group = mod3.shape[1] - 1
    lat_tiles = n_lat // tt
    mod_map = lambda i, j: (li, jnp.where(j < lat_tiles, i, ctx_group), 0, 0, 0)
    kern = functools.partial(_proj_kernel, kv_lora=kv_lora, q_lora=q_lora, bw=bw, n_tok=len(toks),
                             lat_tiles=lat_tiles)
    names = ["norm_g", "w1", "kv_norm", "wk", "wv", "q_norm", "wq", "wqp", "qgain", "qgainp", "kgain", "kgainp"]
    return pl.pallas_call(
        kern,
        out_shape=[jax.ShapeDtypeStruct((b, HEADS, t, HEAD_PAD), BF16),
                   jax.ShapeDtypeStruct((b, HEADS, t, HEAD_PAD), BF16),
                   jax.ShapeDtypeStruct((b, HEADS, VDIM, t), BF16),
                   jax.ShapeDtypeStruct((t, b, bw), F32),
                   jax.ShapeDtypeStruct((b, t, 2 * bw), F32)],
        grid=(b // nb, t // tt),
        in_specs=_token_specs(toks, nb, tt, lat_tiles)
                 + [pl.BlockSpec((None, None, nb, 3, d), mod_map)]
                 + [_wspec(w[n], (li,)) for n in names]
                 + [pl.BlockSpec((tt, HEAD_PAD), lambda i, j: (j, 0)),
                    pl.BlockSpec((tt, HEAD_PAD), lambda i, j: (j, 0))],
        out_specs=[pl.BlockSpec((nb, HEADS, tt, HEAD_PAD), lambda i, j: (i, 0, j, 0)),
                   pl.BlockSpec((nb, HEADS, tt, HEAD_PAD), lambda i, j: (i, 0, j, 0)),
                   pl.BlockSpec((nb, HEADS, VDIM, tt), lambda i, j: (i, 0, 0, j)),
                   pl.BlockSpec((tt, nb, bw), lambda i, j: (j, i, 0)),
                   pl.BlockSpec((nb, tt, 2 * bw), lambda i, j: (i, j, 0))],
        compiler_params=_params(("arbitrary", "arbitrary")),
        name="proj_qkv",
    )(*toks, mod3, *[w[n] for n in names], cos_t, sin_t)


def _seq_kernel(xp_ref, cw_ref, cb_ref, pw_ref, pb_ref, ps_ref, xl_out, op_out, pad_ref, *, segs, bw):
    zeros = jnp.zeros((HALO, 2 * bw), F32)
    cw = cw_ref[...]
    left = LRU_CONV // 2
    group_w = bw // len(POOL_WINDOWS)
    for lo, t in segs:
        pad_ref[0:HALO, :] = zeros
        pad_ref[HALO + t:2 * HALO + t, :] = zeros
        pad_ref[HALO:HALO + t, :] = xp_ref[0, lo:lo + t, :]

        def shifted(s, c0, c1, t=t):
            return pad_ref[HALO + s:HALO + s + t, c0:c1]

        acc = cb_ref[...] + shifted(-left, 0, bw) * cw[0:1]
        for k in range(1, LRU_CONV):
            acc = acc + shifted(k - left, 0, bw) * cw[k:k + 1]
        xl_out[0, lo:lo + t, :] = acc

        xc = shifted(0, bw, 2 * bw)
        lane = lax.broadcasted_iota(jnp.int32, (t, bw), 1)
        row = lax.broadcasted_iota(jnp.int32, (t, bw), 0)
        s = shifted(-1, bw, 2 * bw) + xc
        sel = s
        half = jnp.ones((t, bw), jnp.int32)
        for gi in range(1, len(POOL_WINDOWS)):
            hw_prev = POOL_WINDOWS[gi - 1] // 2
            hw = POOL_WINDOWS[gi] // 2
            for k in range(hw_prev, hw):
                s = s + shifted(-(k + 1), bw, 2 * bw) + shifted(k, bw, 2 * bw)
            in_group = lane >= gi * group_w
            sel = jnp.where(in_group, s, sel)
            half = jnp.where(in_group, hw, half)
        cnt = (jnp.minimum(row + half, t) - jnp.maximum(row - half, 0)).astype(F32)
        p = sel / cnt - xc
        y = _dot(p.astype(BF16), pw_ref[...]) + pb_ref[...]
        op_out[0, lo:lo + t, :] = y * ps_ref[...]


def _seq(xp, w, li, segs):
    b, t, w2 = xp.shape
    bw = w2 // 2
    kern = functools.partial(_seq_kernel, segs=segs, bw=bw)
    names = ["conv_w", "conv_b", "pool_w", "pool_b", "pool_scale"]
    max_t = max(n for _, n in segs)
    return pl.pallas_call(
        kern,
        out_shape=[jax.ShapeDtypeStruct((b, t, bw), F32), jax.ShapeDtypeStruct((b, t, bw), F32)],
        grid=(b,),
        in_specs=[pl.BlockSpec((1, t, w2), lambda i: (i, 0, 0))] + [_wspec(w[n], (li,)) for n in names],
        out_specs=[pl.BlockSpec((1, t, bw), lambda i: (i, 0, 0)),
                   pl.BlockSpec((1, t, bw), lambda i: (i, 0, 0))],
        scratch_shapes=[pltpu.VMEM((max_t + 2 * HALO, w2), F32)],
        compiler_params=_params(("arbitrary",)),
        name="conv_pool",
    )(xp, *[w[n] for n in names])


def _attn_kernel(q_ref, k_ref, vt_ref, o_ref, s_buf):
    def scores(hh, slot):
        sc = _dot_nt(k_ref[0, hh], q_ref[0, hh])
        s_buf[slot] = sc
        return jnp.max(sc, axis=0, keepdims=True)

    def weighted_values(hh, slot, m):
        e = jnp.exp2(s_buf[slot] - m)
        den = jnp.sum(e, axis=0, keepdims=True)
        return _dot(vt_ref[0, hh], e.astype(BF16)) / den

    outs = []
    m_cur = scores(0, 0)
    for hh in range(HEADS):
        m_next = scores(hh + 1, (hh + 1) % 2) if hh + 1 < HEADS else None
        outs.append(weighted_values(hh, hh % 2, m_cur))
        m_cur = m_next
    o_ref[0] = jnp.concatenate(outs, axis=0).T


def _attend(q, k, vt, n_q, q_off, n_k, k_off, tq):
    b = q.shape[0]
    hv = HEADS * VDIM
    assert n_q % tq == 0 and q_off % tq == 0 and k_off % n_k == 0
    qo, ko = q_off // tq, k_off // n_k
    return pl.pallas_call(
        _attn_kernel,
        out_shape=jax.ShapeDtypeStruct((b, n_q, hv), F32),
        grid=(b, n_q // tq),
        in_specs=[pl.BlockSpec((1, HEADS, tq, HEAD_PAD), lambda i, j: (i, 0, j + qo, 0)),
                  pl.BlockSpec((1, HEADS, n_k, HEAD_PAD), lambda i, j: (i, 0, ko, 0)),
                  pl.BlockSpec((1, HEADS, VDIM, n_k), lambda i, j: (i, 0, 0, ko))],
        out_specs=pl.BlockSpec((1, tq, hv), lambda i, j: (i, j, 0)),
        scratch_shapes=[pltpu.VMEM((2, n_k, tq), F32)],
        compiler_params=_params(("arbitrary", "arbitrary")),
        name="attention",
    )(q, k, vt)


def _scan_kernel(uf_ref, ub_ref, xf_ref, xb_ref, bmat_ref, cre_ref, cim_ref, ar_ref, ai_ref,
                 wa_ref, ba_ref, wx_ref, bx_ref, lam_ref, yf_ref, yb_ref,
                 hbuf, abuf, bbuf, hr_s, hi_s, hl_s, *, tc, bsz, bw, ns):
    r = tc * bsz
    sb = SCAN_SUB
    nsub = tc // sb
    rs = sb * bsz
    nblk = bw // LANE
    sblk = ns // nblk
    u_refs, x_refs, y_refs = (uf_ref, ub_ref), (xf_ref, xb_ref), (yf_ref, yb_ref)

    @pl.when(pl.program_id(0) == 0)
    def _():
        hr_s[...] = jnp.zeros_like(hr_s)
        hi_s[...] = jnp.zeros_like(hi_s)
        hl_s[...] = jnp.zeros_like(hl_s)

    u16 = [u_refs[d][...].reshape(r, bw).astype(BF16) for d in range(2)]
    xls = [jnp.transpose(x_refs[d][...], (1, 0, 2)).reshape(r, bw) for d in range(2)]
    softplus = []
    for d in range(2):
        nl = -lam_ref[d]
        softplus.append(jnp.maximum(nl, 0.0) + jnp.log1p(jnp.exp(-jnp.abs(nl))))

    def sub_rows(d, k):
        kk = (nsub - 1 - k) if d else k
        return kk, slice(kk * rs, (kk + 1) * rs)

    def input_stage(k):
        for d in range(2):
            _, rw = sub_rows(d, k)
            hbuf[d, rw, :] = _dot(u16[d][rw], bmat_ref[d])
            xl = xls[d][rw]
            xb = xl.astype(BF16)
            rg = jax.nn.sigmoid(_dot(xb, wa_ref[d]) + ba_ref[d])
            ig = jax.nn.sigmoid(_dot(xb, wx_ref[d]) + bx_ref[d])
            log_a = (-LRU_C) * rg * softplus[d]
            abuf[d, rw, :] = jnp.exp(log_a)
            th = jnp.tanh(log_a)
            gain2 = -2.0 * th / (1.0 - th)
            gain = jnp.where(gain2 > 0.0, gain2 * lax.rsqrt(gain2), 0.0)
            bbuf[d, rw, :] = gain * (ig * xl)

    w = ns // SCAN_LANE_SPLITS
    lanes = [(slice(j * w, (j + 1) * w), slice(ns + j * w, ns + (j + 1) * w)) for j in range(SCAN_LANE_SPLITS)]
    coef = [[(jnp.broadcast_to(ar_ref[d, :, re_sl], (bsz, w)), jnp.broadcast_to(ai_ref[d, :, re_sl], (bsz, w)))
             for re_sl, _ in lanes] for d in range(2)]
    hr = [[hr_s[d, :, re_sl] for re_sl, _ in lanes] for d in range(2)]
    hi = [[hi_s[d, :, re_sl] for re_sl, _ in lanes] for d in range(2)]
    hl = [hl_s[d] for d in range(2)]

    def scan_stage(k):
        for t in range(k * sb, (k + 1) * sb):
            for d in range(2):
                te = (tc - 1 - t) if d else t
                rw = slice(te * bsz, (te + 1) * bsz)
                hl[d] = abuf[d, rw, :] * hl[d] + bbuf[d, rw, :]
                y_refs[d][te, :, bw:2 * bw] = hl[d]
                for j, (re_sl, im_sl) in enumerate(lanes):
                    ar, ai = coef[d][j]
                    nhr = ar * hr[d][j] - ai * hi[d][j] + hbuf[d, rw, re_sl]
                    nhi = ar * hi[d][j] + ai * hr[d][j] + hbuf[d, rw, im_sl]
                    hbuf[d, rw, re_sl] = nhr
                    hbuf[d, rw, im_sl] = nhi
                    hr[d][j], hi[d][j] = nhr, nhi

    def readout_stage(k):
        for d in range(2):
            kk, rw = sub_rows(d, k)
            for jb in range(nblk):
                st = slice(jb * sblk, (jb + 1) * sblk)
                sti = slice(ns + jb * sblk, ns + (jb + 1) * sblk)
                ln = slice(jb * LANE, (jb + 1) * LANE)
                y = (_dot(hbuf[d, rw, st].astype(BF16), cre_ref[d, st, ln])
                     - _dot(hbuf[d, rw, sti].astype(BF16), cim_ref[d, st, ln]))
                y_refs[d][kk * sb:(kk + 1) * sb, :, ln] = y.reshape(sb, bsz, LANE)

    lead = min(2, nsub)
    for k in range(lead):
        input_stage(k)
    for k in range(nsub):
        if k + lead < nsub:
            input_stage(k + lead)
        scan_stage(k)
        if k >= 1:
            readout_stage(k - 1)
    readout_stage(nsub - 1)

    for d in range(2):
        hl_s[d] = hl[d]
        for j, (re_sl, _) in enumerate(lanes):
            hr_s[d, :, re_sl] = hr[d][j]
            hi_s[d, :, re_sl] = hi[d][j]


def _scan(u_tm, xl, w, li, n_lat):
    t, bsz, bw = u_tm.shape
    tc = SCAN_CHUNK
    nc = t // tc
    ncl = n_lat // tc
    ncc = nc - ncl
    ns = w["ar"].shape[-1]
    fwd = lambda s: jnp.where(s < ncc, ncl + s, s - ncc)
    bwd = lambda s: nc - 1 - s
    names = ["bmat", "cre", "cim", "ar", "ai", "wa", "ba", "wx", "bx", "lam"]
    r = tc * bsz
    kern = functools.partial(_scan_kernel, tc=tc, bsz=bsz, bw=bw, ns=ns)
    return pl.pallas_call(
        kern,
        out_shape=[jax.ShapeDtypeStruct((t, bsz, 2 * bw), F32)] * 2,
        grid=(nc,),
        in_specs=[pl.BlockSpec((tc, bsz, bw), lambda s: (fwd(s), 0, 0)),
                  pl.BlockSpec((tc, bsz, bw), lambda s: (bwd(s), 0, 0)),
                  pl.BlockSpec((bsz, tc, bw), lambda s: (0, fwd(s), 0)),
                  pl.BlockSpec((bsz, tc, bw), lambda s: (0, bwd(s), 0))]
                 + [_wspec(w[n], (li,)) for n in names],
        out_specs=[pl.BlockSpec((tc, bsz, 2 * bw), lambda s: (fwd(s), 0, 0)),
                   pl.BlockSpec((tc, bsz, 2 * bw), lambda s: (bwd(s), 0, 0))],
        scratch_shapes=[pltpu.VMEM((2, r, 2 * ns), F32), pltpu.VMEM((2, r, bw), F32), pltpu.VMEM((2, r, bw), F32),
                        pltpu.VMEM((2, bsz, ns), F32), pltpu.VMEM((2, bsz, ns), F32),
                        pltpu.VMEM((2, bsz, bw), F32)],
        compiler_params=_params(("arbitrary",)),
        name="scan_bidir",
    )(u_tm, u_tm, xl, xl, *[w[n] for n in names])


def _merge_kernel(*refs, bw, lat_tiles, with_ctx, n_tok):
    tok_refs, refs = refs[:n_tok], refs[n_tok:]
    if with_ctx:
        (mod_ref, ng_ref, om_ref, omc_ref, yf_ref, yb_ref, u_ref, op_ref,
         w2_ref, wb_ref, wo_ref, wglu_ref, d_ref, out_ref) = refs
    else:
        (mod_ref, ng_ref, om_ref, yf_ref, yb_ref, u_ref, op_ref,
         w2_ref, wb_ref, wo_ref, wglu_ref, d_ref, out_ref) = refs
    nb, tt, d = tok_refs[0].shape
    rows = nb * tt
    x = _read_tokens(tok_refs, lat_tiles)
    mod = mod_ref[...]
    shift, scale, gate = mod[:, 0:1, :], mod[:, 1:2, :], mod[:, 2:3, :]
    h = (_rms(x, ng_ref[...]) * (1.0 + scale) + shift).reshape(rows, d).astype(BF16)

    rr = jnp.transpose(yf_ref[...] + yb_ref[...], (1, 0, 2)).reshape(rows, 2 * bw)
    u = jnp.transpose(u_ref[...], (1, 0, 2)).reshape(rows, bw)
    g = jax.nn.gelu(rr[:, 0:bw] + d_ref[...] * u, approximate=True)
    o_s5 = g * jax.nn.sigmoid(_dot(g.astype(BF16), wglu_ref[...]))
    o_mla = om_ref[...]
    if with_ctx:
        o_mla = jnp.where(pl.program_id(1) >= lat_tiles, omc_ref[...], o_mla)
    branches = (o_mla.reshape(rows, bw), o_s5, rr[:, bw:2 * bw], op_ref[...].reshape(rows, bw))

    y = None
    for n in range(N_BRANCH):
        gp = _dot(h, w2_ref[:, n * bw:(n + 1) * bw])
        off = N_BRANCH * bw + n * d
        ml = _dot(h, w2_ref[:, off:off + d])
        tn = _dot((branches[n] * (gp * jax.nn.sigmoid(gp))).astype(BF16), wb_ref[n])
        term = jax.nn.sigmoid(ml) * tn
        y = term if y is None else y + term
    out_ref[...] = x + gate * _dot(y.astype(BF16), wo_ref[...]).reshape(nb, tt, d)


def _merge(toks, mod3, o_lat, o_ctx, yf, yb, u_tm, o_pool, w, li, n_lat, n_out):
    b, _, d = toks[0].shape
    nb, tt = SUBLANE, MERGE_STEPS
    bw = d // N_BRANCH
    with_ctx = o_ctx is not None
    lat_tiles = n_lat // tt
    ctx_group = mod3.shape[1] - 1
    mod_map = lambda i, j: (li, jnp.where(j < lat_tiles, i, ctx_group), 0, 0, 0)
    bm = lambda width: pl.BlockSpec((nb, tt, width), lambda i, j: (i, j, 0))
    tm = lambda width: pl.BlockSpec((tt, nb, width), lambda i, j: (j, i, 0))
    names = ["w2", "wb", "wo", "wglu", "s5_d"]
    in_specs = _token_specs(toks, nb, tt, lat_tiles) + [
        pl.BlockSpec((None, None, nb, 3, d), mod_map), _wspec(w["norm_g"], (li,)),
        pl.BlockSpec((nb, tt, bw), lambda i, j: (i, jnp.minimum(j, lat_tiles - 1), 0))]
    args = [*toks, mod3, w["norm_g"], o_lat]
    if with_ctx:
        ctx_tiles = o_ctx.shape[1] // tt
        in_specs.append(pl.BlockSpec((nb, tt, bw), lambda i, j: (i, jnp.clip(j - lat_tiles, 0, ctx_tiles - 1), 0)))
        args.append(o_ctx)
    in_specs += [tm(2 * bw), tm(2 * bw), tm(bw), bm(bw)] + [_wspec(w[n], (li,)) for n in names]
    args += [yf, yb, u_tm, o_pool] + [w[n] for n in names]
    kern = functools.partial(_merge_kernel, bw=bw, lat_tiles=lat_tiles, with_ctx=with_ctx, n_tok=len(toks))
    return pl.pallas_call(
        kern,
        out_shape=jax.ShapeDtypeStruct((b, n_out, d), F32),
        grid=(b // nb, n_out // tt),
        in_specs=in_specs,
        out_specs=bm(d),
        compiler_params=_params(("arbitrary", "arbitrary")),
        name="merge_out",
    )(*args)


def _block_diag(blocks):
    n, a, b = blocks.shape[-3:]
    eye = jnp.eye(n, dtype=blocks.dtype)
    return jnp.einsum("...nab,nm->...namb", blocks, eye).reshape(blocks.shape[:-3] + (n * a, n * b))


def _rope_perm():
    half = ROPE // 2
    nf = half // 2
    base = jnp.concatenate([jnp.arange(nf, half), jnp.arange(0, nf)])
    return jnp.concatenate([base, base + half])


def _head_pad(a, lo):
    n = a.shape[-1]
    pad = [(0, 0)] * (a.ndim - 1) + [(lo, HEAD_PAD - lo - n)]
    return jnp.pad(a, pad)


def _rope_tables(n_lat, n_ctx):
    rows_n = n_lat // GRID_W
    row = jnp.repeat(jnp.arange(rows_n, dtype=jnp.int32), GRID_W).astype(F32)
    col = jnp.tile(jnp.arange(GRID_W, dtype=jnp.int32), rows_n).astype(F32)
    nf = ROPE // 4
    inv = ROPE_THETA ** (-jnp.arange(nf, dtype=F32) / nf)
    ang_r = row[:, None] * inv
    ang_c = col[:, None] * inv
    cr, sr, cc, sc = jnp.cos(ang_r), jnp.sin(ang_r), jnp.cos(ang_c), jnp.sin(ang_c)
    ones = jnp.ones((n_lat, NOPE), F32)
    zeros = jnp.zeros((n_lat, NOPE), F32)
    tail = HEAD_PAD - QK
    cos = jnp.concatenate([ones, cr, cr, cc, cc, jnp.ones((n_lat, tail), F32)], -1)
    sin = jnp.concatenate([zeros, -sr, sr, -sc, sc, jnp.zeros((n_lat, tail), F32)], -1)
    cos = jnp.concatenate([cos, jnp.ones((n_ctx, HEAD_PAD), F32)], 0)
    sin = jnp.concatenate([sin, jnp.zeros((n_ctx, HEAD_PAD), F32)], 0)
    return cos, sin


def _prep_weights(p, disc):
    w_in = p["w_in"]
    depth, d, _ = w_in.shape
    bw = d // N_BRANCH
    kv_lora = p["mla_kv_norm"].shape[-1]
    q_lora = p["mla_q_norm"].shape[-1]
    perm = _rope_perm()
    off_krope = kv_lora
    off_s5 = off_krope + ROPE
    off_lru = off_s5 + bw
    off_cq = off_lru + bw
    off_pool = off_cq + q_lora
    off_gate = off_pool + bw
    w_krope = w_in[:, :, off_krope:off_s5]
    w1 = jnp.concatenate([
        w_in[:, :, :kv_lora],
        _head_pad(w_krope, NOPE),
        _head_pad(w_krope[:, :, perm], NOPE),
        w_in[:, :, off_cq:off_pool],
        w_in[:, :, off_s5:off_lru],
        w_in[:, :, off_lru:off_cq],
        w_in[:, :, off_pool:off_gate]], axis=2).astype(BF16)
    w2 = w_in[:, :, off_gate:].astype(BF16)

    w_uq = p["mla_w_uq"].reshape(depth, q_lora, HEADS, QK)
    wq = _head_pad(w_uq, 0).reshape(depth, q_lora, HEADS * HEAD_PAD).astype(BF16)
    wqp = _head_pad(w_uq[..., NOPE:][..., perm], NOPE).reshape(depth, q_lora, HEADS * HEAD_PAD).astype(BF16)
    w_ukv = p["mla_w_ukv"].reshape(depth, kv_lora, HEADS, NOPE + VDIM)
    wk = _head_pad(w_ukv[..., :NOPE], 0).reshape(depth, kv_lora, HEADS * HEAD_PAD).astype(BF16)
    wv = w_ukv[..., NOPE:].reshape(depth, kv_lora, HEADS * VDIM).astype(BF16)
    q_gain, k_gain = p["mla_q_gain"], p["mla_k_gain"]
    row = lambda a: a[..., None, :]

    ab_re, ab_im, bb_re, bb_im = disc
    sw = lambda a: jnp.swapaxes(a, -1, -2)
    bmat = jnp.concatenate([_block_diag(sw(bb_re)), _block_diag(sw(bb_im))], axis=-1).astype(BF16)
    return dict(
        norm_g=row(p["norm_g"]), w1=w1, w2=w2,
        kv_norm=row(p["mla_kv_norm"]), q_norm=row(p["mla_q_norm"]),
        wq=wq, wqp=wqp, wk=wk, wv=wv,
        qgain=row(_head_pad(q_gain, 0)), qgainp=row(_head_pad(q_gain[:, NOPE:][:, perm], NOPE)),
        kgain=row(_head_pad(k_gain, 0)), kgainp=row(_head_pad(k_gain[:, NOPE:][:, perm], NOPE)),
        conv_w=p["lru_conv_w"], conv_b=row(p["lru_conv_b"]),
        pool_w=_block_diag(p["pool_w"]).astype(BF16), pool_b=row(p["pool_b"]),
        pool_scale=row(p["pool_scale"]),
        wb=p["w_branch"].astype(BF16), wo=p["w_out"].astype(BF16),
        wglu=p["s5_w_glu"].astype(BF16), s5_d=row(p["s5_d"]),
        bmat=bmat,
        cre=_block_diag(sw(p["s5_c_re"])).astype(BF16), cim=_block_diag(sw(p["s5_c_im"])).astype(BF16),
        ar=row(ab_re.reshape(ab_re.shape[:2] + (-1,))), ai=row(ab_im.reshape(ab_im.shape[:2] + (-1,))),
        wa=_block_diag(p["lru_w_a"]).astype(BF16), ba=row(p["lru_b_a"]),
        wx=_block_diag(p["lru_w_x"]).astype(BF16), bx=row(p["lru_b_x"]),
        lam=row(p["lru_lambda"]))


def kernel(x, c, ctx, c_ctx, w_ada, b_ada, norm_g, w_in, mla_q_norm, mla_kv_norm, mla_w_uq, mla_w_ukv, mla_q_gain, mla_k_gain, s5_a_re, s5_a_im, s5_log_dt, s5_b_re, s5_b_im, s5_c_re, s5_c_im, s5_d, s5_w_glu, lru_conv_w, lru_conv_b, lru_lambda, lru_w_a, lru_b_a, lru_w_x, lru_b_x, pool_w, pool_b, pool_scale, w_branch, w_out):
    p = dict(w_in=w_in, norm_g=norm_g, mla_q_norm=mla_q_norm, mla_kv_norm=mla_kv_norm,
             mla_w_uq=mla_w_uq, mla_w_ukv=mla_w_ukv, mla_q_gain=mla_q_gain, mla_k_gain=mla_k_gain,
             s5_c_re=s5_c_re, s5_c_im=s5_c_im, s5_d=s5_d, s5_w_glu=s5_w_glu,
             lru_conv_w=lru_conv_w, lru_conv_b=lru_conv_b, lru_lambda=lru_lambda,
             lru_w_a=lru_w_a, lru_b_a=lru_b_a, lru_w_x=lru_w_x, lru_b_x=lru_b_x,
             pool_w=pool_w, pool_b=pool_b, pool_scale=pool_scale, w_branch=w_branch, w_out=w_out)
    depth = w_in.shape[0]
    b, l, d = x.shape
    lc = ctx.shape[1]
    t = l + lc
    assert b % SUBLANE == 0 and l % PROJ_STEPS == 0 and lc % PROJ_STEPS == 0 and l % lc == 0

    n_rows = -(-(b + 1) // SUBLANE) * SUBLANE
    cc = jnp.concatenate([c, c_ctx[None], jnp.zeros((n_rows - b - 1, d), F32)], axis=0)
    mod_all = _ada(cc, w_ada, b_ada)
    mod3 = jnp.concatenate([
        mod_all[:, :b].reshape(depth, b // SUBLANE, SUBLANE, 3, d),
        jnp.broadcast_to(mod_all[:, b:b + 1].reshape(depth, 1, 1, 3, d), (depth, 1, SUBLANE, 3, d))], axis=1)
    disc = _s5_discretize(s5_a_re, s5_a_im, s5_log_dt, s5_b_re, s5_b_im)
    w = _prep_weights(p, disc)
    cos_t, sin_t = _rope_tables(l, lc)
    tq_l, tq_c = min(Q_TILE, l), min(Q_TILE, lc)

    toks = (x, ctx)
    for li in range(depth):
        with_ctx = li < depth - 1
        q, k, vt, u_tm, xp = _proj(toks, mod3, w, li, cos_t, sin_t, l)
        xl, o_pool = _seq(xp, w, li, ((0, l), (l, lc)))
        o_lat = _attend(q, k, vt, l, 0, t, 0, tq_l)
        o_ctx = _attend(q, k, vt, lc, l, lc, l, tq_c) if with_ctx else None
        yf, yb = _scan(u_tm, xl, w, li, l)
        toks = (_merge(toks, mod3, o_lat, o_ctx, yf, yb, u_tm, o_pool, w, li, l, t if with_ctx else l),)
    return toks[0]
```

```python
import functools
import math

import jax
import jax.numpy as jnp
from jax import lax
from jax.experimental import pallas as pl
from jax.experimental.pallas import tpu as pltpu

F32 = jnp.float32
BF16 = jnp.bfloat16

EPS = 1e-6
GRID_W = 64
N_BRANCH = 4
HEADS = 4
NOPE = 64
ROPE = 32
VDIM = 64
QK = NOPE + ROPE
ROPE_THETA = 10000.0
LRU_CONV = 4
LRU_C = 8.0
POOL_WINDOWS = (2, 4, 8, 16)
Q_SCALE = (QK ** -0.5) * math.log2(math.e)

LANE = 128
SUBLANE = 8
HEAD_PAD = LANE
VMEM_LIMIT = 56 * 1024 * 1024

PROJ_STEPS = 128
MERGE_STEPS = 64
Q_TILE = 512
KEY_CHUNK = 256
ATTN_LOOKAHEAD = 2
SCAN_CHUNK = 64
SCAN_SUB = 8
SCAN_LANE_SPLITS = 2
HALO = 8


def _dot(a, b):
    return jnp.dot(a, b, preferred_element_type=F32)


def _dot_nt(a, b):
    return lax.dot_general(a, b, (((1,), (1,)), ((), ())), preferred_element_type=F32)


def _rms(x, g):
    return x * lax.rsqrt(jnp.mean(x * x, axis=-1, keepdims=True) + EPS) * g


def _wspec(arr, lead):
    tail = arr.shape[len(lead):]
    idx = tuple(lead) + (0,) * len(tail)
    return pl.BlockSpec((None,) * len(lead) + tail, lambda *_: idx, pipeline_mode=pl.Buffered(1))


def _token_specs(toks, nb, tt, lat_tiles):
    if len(toks) == 1:
        return [pl.BlockSpec((nb, tt, toks[0].shape[2]), lambda i, j: (i, j, 0))]
    lat, ctx = toks
    ctx_tiles = ctx.shape[1] // tt
    return [pl.BlockSpec((nb, tt, lat.shape[2]), lambda i, j: (i, jnp.minimum(j, lat_tiles - 1), 0)),
            pl.BlockSpec((nb, tt, ctx.shape[2]), lambda i, j: (i, jnp.clip(j - lat_tiles, 0, ctx_tiles - 1), 0))]


def _read_tokens(tok_refs, lat_tiles):
    if len(tok_refs) == 1:
        return tok_refs[0][...]
    return jnp.where(pl.program_id(1) >= lat_tiles, tok_refs[1][...], tok_refs[0][...])


def _params(sem):
    return pltpu.CompilerParams(dimension_semantics=sem, vmem_limit_bytes=VMEM_LIMIT)


def _ada_kernel(c_ref, w_ref, b_ref, o_ref):
    c = c_ref[...]
    act = c * jax.nn.sigmoid(c)
    o_ref[0] = jnp.dot(act, w_ref[0], preferred_element_type=F32,
                       precision=lax.Precision.HIGHEST) + b_ref[0]


def _ada(cc, w_ada, b_ada):
    depth, d, d3 = w_ada.shape
    rows = cc.shape[0]
    col = d
    return pl.pallas_call(
        _ada_kernel,
        out_shape=jax.ShapeDtypeStruct((depth, rows, d3), F32),
        grid=(depth, d3 // col),
        in_specs=[pl.BlockSpec((rows, d), lambda l, j: (0, 0)),
                  pl.BlockSpec((1, d, col), lambda l, j: (l, 0, j)),
                  pl.BlockSpec((1, 1, col), lambda l, j: (l, 0, j))],
        out_specs=pl.BlockSpec((1, rows, col), lambda l, j: (l, 0, j)),
        compiler_params=_params(("arbitrary", "arbitrary")),
        name="ada_mod",
    )(cc, w_ada, b_ada.reshape(depth, 1, d3))


def _s5_disc_kernel(are_ref, aim_ref, ldt_ref, bre_ref, bim_ref, abre_ref, abim_ref, bbre_ref, bbim_ref):
    a_re = are_ref[...]
    a_im = aim_ref[...]
    dt = jnp.exp(ldt_ref[...])
    mag = jnp.exp(a_re * dt)
    ab_re = mag * jnp.cos(a_im * dt)
    ab_im = mag * jnp.sin(a_im * dt)
    den = a_re * a_re + a_im * a_im
    f_re = ((ab_re - 1.0) * a_re + ab_im * a_im) / den
    f_im = (ab_im * a_re - (ab_re - 1.0) * a_im) / den
    b_re = bre_ref[...]
    b_im = bim_ref[...]
    abre_ref[...] = ab_re
    abim_ref[...] = ab_im
    bbre_ref[...] = f_re * b_re - f_im * b_im
    bbim_ref[...] = f_re * b_im + f_im * b_re


def _s5_discretize(a_re, a_im, log_dt, b_re, b_im):
    full = b_re.shape
    rows = b_re.size // LANE
    expand = lambda a: jnp.broadcast_to(a[..., None], full).reshape(rows, LANE)
    ins = [expand(a_re), expand(a_im), expand(jnp.broadcast_to(log_dt[..., None], a_re.shape)),
           b_re.reshape(rows, LANE), b_im.reshape(rows, LANE)]
    outs = pl.pallas_call(
        _s5_disc_kernel,
        out_shape=[jax.ShapeDtypeStruct((rows, LANE), F32)] * 4,
        name="s5_discretize",
    )(*ins)
    ab_re, ab_im, bb_re, bb_im = [o.reshape(full) for o in outs]
    return ab_re[..., 0], ab_im[..., 0], bb_re, bb_im


def _proj_kernel(*refs, kv_lora, q_lora, bw, n_tok, lat_tiles):
    tok_refs = refs[:n_tok]
    (mod_ref, ng_ref, w1_ref, kvg_ref, wk_ref, wv_ref, qg_ref, wq_ref, wqp_ref,
     qgain_ref, qgainp_ref, kgain_ref, kgainp_ref, cos_ref, sin_ref,
     q_out, k_out, vt_out, u_out, xp_out) = refs[n_tok:]
    nb, tt, d = tok_refs[0].shape
    rows = nb * tt
    mod = mod_ref[...]
    shift, scale = mod[:, 0:1, :], mod[:, 1:2, :]
    h = _rms(_read_tokens(tok_refs, lat_tiles), ng_ref[...]) * (1.0 + scale) + shift
    z = _dot(h.reshape(rows, d).astype(BF16), w1_ref[...])

    o = 0
    ckv = z[:, o:o + kv_lora]; o += kv_lora
    kr = z[:, o:o + HEAD_PAD]; o += HEAD_PAD
    kp = z[:, o:o + HEAD_PAD]; o += HEAD_PAD
    cq = z[:, o:o + q_lora]; o += q_lora
    u_out[...] = jnp.transpose(z[:, o:o + bw].reshape(nb, tt, bw), (1, 0, 2)); o += bw
    xp_out[...] = z[:, o:o + 2 * bw].reshape(nb, tt, 2 * bw)

    cos = jnp.concatenate([cos_ref[...]] * nb, axis=0)
    sin = jnp.concatenate([sin_ref[...]] * nb, axis=0)
    inv_qk = 1.0 / QK

    cqn = _rms(cq, qg_ref[...]).astype(BF16)
    qraw = _dot(cqn, wq_ref[...])
    qprt = _dot(cqn, wqp_ref[...])
    qgain, qgainp = qgain_ref[...], qgainp_ref[...]
    for hh in range(HEADS):
        sl = slice(hh * HEAD_PAD, (hh + 1) * HEAD_PAD)
        qh = qraw[:, sl]
        rs = lax.rsqrt(jnp.sum(qh * qh, axis=-1, keepdims=True) * inv_qk + EPS)
        qn = qh * rs * qgain
        qpn = qprt[:, sl] * rs * qgainp
        q_out[:, hh] = ((qn * cos + qpn * sin) * Q_SCALE).astype(BF16).reshape(nb, tt, HEAD_PAD)

    ckvn = _rms(ckv, kvg_ref[...]).astype(BF16)
    knope = _dot(ckvn, wk_ref[...])
    v = _dot(ckvn, wv_ref[...])
    kgain, kgainp = kgain_ref[...], kgainp_ref[...]
    for hh in range(HEADS):
        sl = slice(hh * HEAD_PAD, (hh + 1) * HEAD_PAD)
        kh = knope[:, sl] + kr
        rs = lax.rsqrt(jnp.sum(kh * kh, axis=-1, keepdims=True) * inv_qk + EPS)
        kn = kh * rs * kgain
        kpn = kp * rs * kgainp
        k_out[:, hh] = (kn * cos + kpn * sin).astype(BF16).reshape(nb, tt, HEAD_PAD)
    for bi in range(nb):
        vt_out[bi] = v[bi * tt:(bi + 1) * tt].T.astype(BF16).reshape(HEADS, VDIM, tt)


def _proj(toks, mod3, w, li, cos_t, sin_t, n_lat):
    b, _, d = toks[0].shape
    t = sum(a.shape[1] for a in toks)
    nb, tt = SUBLANE, PROJ_STEPS
    bw = d // N_BRANCH
    kv_lora, q_lora = w["kv_norm"].shape[-1], w["q_norm"].shape[-1]
    ctx_group = mod3.shape[1] - 1
    lat_tiles = n_lat // tt
    mod_map = lambda i, j: (li, jnp.where(j < lat_tiles, i, ctx_group), 0, 0, 0)
    kern = functools.partial(_proj_kernel, kv_lora=kv_lora, q_lora=q_lora, bw=bw, n_tok=len(toks),
                             lat_tiles=lat_tiles)
    names = ["norm_g", "w1", "kv_norm", "wk", "wv", "q_norm", "wq", "wqp", "qgain", "qgainp", "kgain", "kgainp"]
    return pl.pallas_call(
        kern,
        out_shape=[jax.ShapeDtypeStruct((b, HEADS, t, HEAD_PAD), BF16),
                   jax.ShapeDtypeStruct((b, HEADS, t, HEAD_PAD), BF16),
                   jax.ShapeDtypeStruct((b, HEADS, VDIM, t), BF16),
                   jax.ShapeDtypeStruct((t, b, bw), F32),
                   jax.ShapeDtypeStruct((b, t, 2 * bw), F32)],
        grid=(b // nb, t // tt),
        in_specs=_token_specs(toks, nb, tt, lat_tiles)
                 + [pl.BlockSpec((None, None, nb, 3, d), mod_map)]
                 + [_wspec(w[n], (li,)) for n in names]
                 + [pl.BlockSpec((tt, HEAD_PAD), lambda i, j: (j, 0)),
                    pl.BlockSpec((tt, HEAD_PAD), lambda i, j: (j, 0))],
        out_specs=[pl.BlockSpec((nb, HEADS, tt, HEAD_PAD), lambda i, j: (i, 0, j, 0)),
                   pl.BlockSpec((nb, HEADS, tt, HEAD_PAD), lambda i, j: (i, 0, j, 0)),
                   pl.BlockSpec((nb, HEADS, VDIM, tt), lambda i, j: (i, 0, 0, j)),
                   pl.BlockSpec((tt, nb, bw), lambda i, j: (j, i, 0)),
                   pl.BlockSpec((nb, tt, 2 * bw), lambda i, j: (i, j, 0))],
        compiler_params=_params(("arbitrary", "arbitrary")),
        name="proj_qkv",
    )(*toks, mod3, *[w[n] for n in names], cos_t, sin_t)


def _seq_kernel(xp_ref, cw_ref, cb_ref, pw_ref, pb_ref, ps_ref, xl_out, op_out,
                pad_ref, f2_ref, f4_ref, f8_ref, inv_ref, *, segs, bw):
    assert POOL_WINDOWS == (2, 4, 8, 16) and bw == 2 * LANE and HALO == 8
    gw = bw // len(POOL_WINDOWS)
    cw = cw_ref[...]
    left = LRU_CONV // 2

    @pl.when(pl.program_id(0) == 0)
    def _():
        for lo, t in segs:
            lane = lax.broadcasted_iota(jnp.int32, (t, bw), 1)
            row = lax.broadcasted_iota(jnp.int32, (t, bw), 0)
            half = jnp.ones((t, bw), jnp.int32)
            for gi in range(1, len(POOL_WINDOWS)):
                half = jnp.where(lane >= gi * gw, POOL_WINDOWS[gi] // 2, half)
            cnt = (jnp.minimum(row + half, t) - jnp.maximum(row - half, 0)).astype(F32)
            inv_ref[lo:lo + t, :] = 1.0 / cnt

    lane_half = lax.broadcasted_iota(jnp.int32, (1, LANE), 1) < gw
    for lo, t in segs:
        n = t + 2 * HALO
        pad_ref[0:HALO, :] = jnp.zeros((HALO, 2 * bw), F32)
        pad_ref[HALO + t:n + HALO, :] = jnp.zeros((2 * HALO, 2 * bw), F32)
        pad_ref[HALO:HALO + t, :] = xp_ref[0, lo:lo + t, :]
        f2_ref[n:n + HALO, :] = jnp.zeros((HALO, bw), F32)
        f4_ref[n:n + HALO, :] = jnp.zeros((HALO, bw), F32)
        f8_ref[n:n + HALO, :] = jnp.zeros((HALO, LANE), F32)

        acc = cb_ref[...] + pad_ref[HALO - left:HALO - left + t, 0:bw] * cw[0:1]
        for k in range(1, LRU_CONV):
            acc = acc + pad_ref[HALO + k - left:HALO + k - left + t, 0:bw] * cw[k:k + 1]
        xl_out[0, lo:lo + t, :] = acc

        f2_ref[0:n, :] = pad_ref[0:n, bw:2 * bw] + pad_ref[1:n + 1, bw:2 * bw]
        f4_ref[0:n, :] = f2_ref[0:n, :] + f2_ref[2:n + 2, :]
        f8_ref[0:n, :] = f4_ref[0:n, LANE:bw] + f4_ref[4:n + 4, LANE:bw]
        sel_lo = jnp.where(lane_half, f2_ref[7:7 + t, 0:LANE], f4_ref[6:6 + t, 0:LANE])
        sel_hi = jnp.where(lane_half, f8_ref[4:4 + t, :], f8_ref[0:t, :] + f8_ref[HALO:HALO + t, :])
        sel = jnp.concatenate([sel_lo, sel_hi], axis=1)
        p = sel * inv_ref[lo:lo + t, :] - pad_ref[HALO:HALO + t, bw:2 * bw]
        y = _dot(p.astype(BF16), pw_ref[...]) + pb_ref[...]
        op_out[0, lo:lo + t, :] = y * ps_ref[...]


def _seq(xp, w, li, segs):
    b, t, w2 = xp.shape
    bw = w2 // 2
    kern = functools.partial(_seq_kernel, segs=segs, bw=bw)
    names = ["conv_w", "conv_b", "pool_w", "pool_b", "pool_scale"]
    max_t = max(n for _, n in segs)
    return pl.pallas_call(
        kern,
        out_shape=[jax.ShapeDtypeStruct((b, t, bw), F32), jax.ShapeDtypeStruct((b, t, bw), F32)],
        grid=(b,),
        in_specs=[pl.BlockSpec((1, t, w2), lambda i: (i, 0, 0))] + [_wspec(w[n], (li,)) for n in names],
        out_specs=[pl.BlockSpec((1, t, bw), lambda i: (i, 0, 0)),
                   pl.BlockSpec((1, t, bw), lambda i: (i, 0, 0))],
        scratch_shapes=[pltpu.VMEM((max_t + 3 * HALO, w2), F32), pltpu.VMEM((max_t + 3 * HALO, bw), F32),
                        pltpu.VMEM((max_t + 3 * HALO, bw), F32), pltpu.VMEM((max_t + 3 * HALO, LANE), F32),
                        pltpu.VMEM((t, bw), F32)],
        compiler_params=_params(("arbitrary",)),
        name="conv_pool",
    )(xp, *[w[n] for n in names])


def _attn_kernel(q_ref, k_ref, vt_ref, o_ref):
    n_k = k_ref.shape[2]
    ck = min(KEY_CHUNK, n_k)
    nch = n_k // ck
    items = [(hh, c) for hh in range(HEADS) for c in range(nch)]

    def scores(item):
        hh, c = item
        return _dot_nt(k_ref[0, hh, c * ck:(c + 1) * ck, :], q_ref[0, hh])

    outs = []
    m = l = acc = None
    ahead = [scores(it) for it in items[:ATTN_LOOKAHEAD]]
    for i, (hh, c) in enumerate(items):
        sc = ahead.pop(0)
        if i + ATTN_LOOKAHEAD < len(items):
            ahead.append(scores(items[i + ATTN_LOOKAHEAD]))
        ks = slice(c * ck, (c + 1) * ck)
        mc = jnp.max(sc, axis=0, keepdims=True)
        if c == 0:
            m_new = mc
            e = jnp.exp2(sc - m_new)
            l = jnp.sum(e, axis=0, keepdims=True)
            acc = _dot(vt_ref[0, hh, :, ks], e.astype(BF16))
        else:
            m_new = jnp.maximum(m, mc)
            alpha = jnp.exp2(m - m_new)
            e = jnp.exp2(sc - m_new)
            l = alpha * l + jnp.sum(e, axis=0, keepdims=True)
            acc = alpha * acc + _dot(vt_ref[0, hh, :, ks], e.astype(BF16))
        m = m_new
        if c == nch - 1:
            outs.append(acc / l)
    o_ref[0] = jnp.concatenate(outs, axis=0).T


def _attend(q, k, vt, n_q, q_off, n_k, k_off, tq):
    b = q.shape[0]
    hv = HEADS * VDIM
    assert n_q % tq == 0 and q_off % tq == 0 and k_off % n_k == 0
    qo, ko = q_off // tq, k_off // n_k
    return pl.pallas_call(
        _attn_kernel,
        out_shape=jax.ShapeDtypeStruct((b, n_q, hv), F32),
        grid=(b, n_q // tq),
        in_specs=[pl.BlockSpec((1, HEADS, tq, HEAD_PAD), lambda i, j: (i, 0, j + qo, 0)),
                  pl.BlockSpec((1, HEADS, n_k, HEAD_PAD), lambda i, j: (i, 0, ko, 0)),
                  pl.BlockSpec((1, HEADS, VDIM, n_k), lambda i, j: (i, 0, 0, ko))],
        out_specs=pl.BlockSpec((1, tq, hv), lambda i, j: (i, j, 0)),
        compiler_params=_params(("arbitrary", "arbitrary")),
        name="attention",
    )(q, k, vt)


def _scan_kernel(uf_ref, ub_ref, xf_ref, xb_ref, bmat_ref, cre_ref, cim_ref, ar_ref, ai_ref,
                 wa_ref, ba_ref, wx_ref, bx_ref, lam_ref, yf_ref, yb_ref,
                 hbuf, abuf, bbuf, hr_s, hi_s, hl_s, *, tc, bsz, bw, ns):
    r = tc * bsz
    sb = SCAN_SUB
    nsub = tc // sb
    rs = sb * bsz
    nblk = bw // LANE
    sblk = ns // nblk
    u_refs, x_refs, y_refs = (uf_ref, ub_ref), (xf_ref, xb_ref), (yf_ref, yb_ref)

    @pl.when(pl.program_id(0) == 0)
    def _():
        hr_s[...] = jnp.zeros_like(hr_s)
        hi_s[...] = jnp.zeros_like(hi_s)
        hl_s[...] = jnp.zeros_like(hl_s)

    u16 = [u_refs[d][...].reshape(r, bw).astype(BF16) for d in range(2)]
    xls = [jnp.transpose(x_refs[d][...], (1, 0, 2)).reshape(r, bw) for d in range(2)]
    softplus = []
    for d in range(2):
        nl = -lam_ref[d]
        softplus.append(jnp.maximum(nl, 0.0) + jnp.log1p(jnp.exp(-jnp.abs(nl))))

    def sub_rows(d, k):
        kk = (nsub - 1 - k) if d else k
        return kk, slice(kk * rs, (kk + 1) * rs)

    def input_stage(k):
        for d in range(2):
            _, rw = sub_rows(d, k)
            hbuf[d, rw, :] = _dot(u16[d][rw], bmat_ref[d])
            xl = xls[d][rw]
            xb = xl.astype(BF16)
            rg = jax.nn.sigmoid(_dot(xb, wa_ref[d]) + ba_ref[d])
            ig = jax.nn.sigmoid(_dot(xb, wx_ref[d]) + bx_ref[d])
            log_a = (-LRU_C) * rg * softplus[d]
            abuf[d, rw, :] = jnp.exp(log_a)
            th = jnp.tanh(log_a)
            gain2 = -2.0 * th / (1.0 - th)
            gain = jnp.where(gain2 > 0.0, gain2 * lax.rsqrt(gain2), 0.0)
            bbuf[d, rw, :] = gain * (ig * xl)

    w = ns // SCAN_LANE_SPLITS
    lanes = [(slice(j * w, (j + 1) * w), slice(ns + j * w, ns + (j + 1) * w)) for j in range(SCAN_LANE_SPLITS)]
    coef = [[(jnp.broadcast_to(ar_ref[d, :, re_sl], (bsz, w)), jnp.broadcast_to(ai_ref[d, :, re_sl], (bsz, w)))
             for re_sl, _ in lanes] for d in range(2)]
    hr = [[hr_s[d, :, re_sl] for re_sl, _ in lanes] for d in range(2)]
    hi = [[hi_s[d, :, re_sl] for re_sl, _ in lanes] for d in range(2)]
    hl = [hl_s[d] for d in range(2)]

    def scan_stage(k):
        for t in range(k * sb, (k + 1) * sb):
            for d in range(2):
                te = (tc - 1 - t) if d else t
                rw = slice(te * bsz, (te + 1) * bsz)
                hl[d] = abuf[d, rw, :] * hl[d] + bbuf[d, rw, :]
                y_refs[d][te, :, bw:2 * bw] = hl[d]
                for j, (re_sl, im_sl) in enumerate(lanes):
                    ar, ai = coef[d][j]
                    nhr = ar * hr[d][j] - ai * hi[d][j] + hbuf[d, rw, re_sl]
                    nhi = ar * hi[d][j] + ai * hr[d][j] + hbuf[d, rw, im_sl]
                    hbuf[d, rw, re_sl] = nhr
                    hbuf[d, rw, im_sl] = nhi
                    hr[d][j], hi[d][j] = nhr, nhi

    def readout_stage(k):
        for d in range(2):
            kk, rw = sub_rows(d, k)
            for jb in range(nblk):
                st = slice(jb * sblk, (jb + 1) * sblk)
                sti = slice(ns + jb * sblk, ns + (jb + 1) * sblk)
                ln = slice(jb * LANE, (jb + 1) * LANE)
                y = (_dot(hbuf[d, rw, st].astype(BF16), cre_ref[d, st, ln])
                     - _dot(hbuf[d, rw, sti].astype(BF16), cim_ref[d, st, ln]))
                y_refs[d][kk * sb:(kk + 1) * sb, :, ln] = y.reshape(sb, bsz, LANE)

    lead = min(2, nsub)
    for k in range(lead):
        input_stage(k)
    for k in range(nsub):
        if k + lead < nsub:
            input_stage(k + lead)
        scan_stage(k)
        if k >= 1:
            readout_stage(k - 1)
    readout_stage(nsub - 1)

    for d in range(2):
        hl_s[d] = hl[d]
        for j, (re_sl, _) in enumerate(lanes):
            hr_s[d, :, re_sl] = hr[d][j]
            hi_s[d, :, re_sl] = hi[d][j]


def _scan(u_tm, xl, w, li, n_lat):
    t, bsz, bw = u_tm.shape
    tc = SCAN_CHUNK
    nc = t // tc
    ncl = n_lat // tc
    ncc = nc - ncl
    ns = w["ar"].shape[-1]
    fwd = lambda s: jnp.where(s < ncc, ncl + s, s - ncc)
    bwd = lambda s: nc - 1 - s
    names = ["bmat", "cre", "cim", "ar", "ai", "wa", "ba", "wx", "bx", "lam"]
    r = tc * bsz
    kern = functools.partial(_scan_kernel, tc=tc, bsz=bsz, bw=bw, ns=ns)
    return pl.pallas_call(
        kern,
        out_shape=[jax.ShapeDtypeStruct((t, bsz, 2 * bw), F32)] * 2,
        grid=(nc,),
        in_specs=[pl.BlockSpec((tc, bsz, bw), lambda s: (fwd(s), 0, 0)),
                  pl.BlockSpec((tc, bsz, bw), lambda s: (bwd(s), 0, 0)),
                  pl.BlockSpec((bsz, tc, bw), lambda s: (0, fwd(s), 0)),
                  pl.BlockSpec((bsz, tc, bw), lambda s: (0, bwd(s), 0))]
                 + [_wspec(w[n], (li,)) for n in names],
        out_specs=[pl.BlockSpec((tc, bsz, 2 * bw), lambda s: (fwd(s), 0, 0)),
                   pl.BlockSpec((tc, bsz, 2 * bw), lambda s: (bwd(s), 0, 0))],
        scratch_shapes=[pltpu.VMEM((2, r, 2 * ns), F32), pltpu.VMEM((2, r, bw), F32), pltpu.VMEM((2, r, bw), F32),
                        pltpu.VMEM((2, bsz, ns), F32), pltpu.VMEM((2, bsz, ns), F32),
                        pltpu.VMEM((2, bsz, bw), F32)],
        compiler_params=_params(("arbitrary",)),
        name="scan_bidir",
    )(u_tm, u_tm, xl, xl, *[w[n] for n in names])


def _merge_kernel(*refs, bw, lat_tiles, with_ctx, n_tok):
    tok_refs, refs = refs[:n_tok], refs[n_tok:]
    if with_ctx:
        (mod_ref, ng_ref, om_ref, omc_ref, yf_ref, yb_ref, u_ref, op_ref,
         w2_ref, wb_ref, wo_ref, wglu_ref, d_ref, out_ref) = refs
    else:
        (mod_ref, ng_ref, om_ref, yf_ref, yb_ref, u_ref, op_ref,
         w2_ref, wb_ref, wo_ref, wglu_ref, d_ref, out_ref) = refs
    nb, tt, d = tok_refs[0].shape
    rows = nb * tt
    x = _read_tokens(tok_refs, lat_tiles)
    mod = mod_ref[...]
    shift, scale, gate = mod[:, 0:1, :], mod[:, 1:2, :], mod[:, 2:3, :]
    h = (_rms(x, ng_ref[...]) * (1.0 + scale) + shift).reshape(rows, d).astype(BF16)

    rr = jnp.transpose(yf_ref[...] + yb_ref[...], (1, 0, 2)).reshape(rows, 2 * bw)
    u = jnp.transpose(u_ref[...], (1, 0, 2)).reshape(rows, bw)
    g = jax.nn.gelu(rr[:, 0:bw] + d_ref[...] * u, approximate=True)
    o_s5 = g * jax.nn.sigmoid(_dot(g.astype(BF16), wglu_ref[...]))
    o_mla = om_ref[...]
    if with_ctx:
        o_mla = jnp.where(pl.program_id(1) >= lat_tiles, omc_ref[...], o_mla)
    branches = (o_mla.reshape(rows, bw), o_s5, rr[:, bw:2 * bw], op_ref[...].reshape(rows, bw))

    y = None
    for n in range(N_BRANCH):
        gp = _dot(h, w2_ref[:, n * bw:(n + 1) * bw])
        off = N_BRANCH * bw + n * d
        ml = _dot(h, w2_ref[:, off:off + d])
        tn = _dot((branches[n] * (gp * jax.nn.sigmoid(gp))).astype(BF16), wb_ref[n])
        term = jax.nn.sigmoid(ml) * tn
        y = term if y is None else y + term
    out_ref[...] = x + gate * _dot(y.astype(BF16), wo_ref[...]).reshape(nb, tt, d)


def _merge(toks, mod3, o_lat, o_ctx, yf, yb, u_tm, o_pool, w, li, n_lat, n_out):
    b, _, d = toks[0].shape
    nb, tt = SUBLANE, MERGE_STEPS
    bw = d // N_BRANCH
    with_ctx = o_ctx is not None
    lat_tiles = n_lat // tt
    ctx_group = mod3.shape[1] - 1
    mod_map = lambda i, j: (li, jnp.where(j < lat_tiles, i, ctx_group), 0, 0, 0)
    bm = lambda width: pl.BlockSpec((nb, tt, width), lambda i, j: (i, j, 0))
    tm = lambda width: pl.BlockSpec((tt, nb, width), lambda i, j: (j, i, 0))
    names = ["w2", "wb", "wo", "wglu", "s5_d"]
    in_specs = _token_specs(toks, nb, tt, lat_tiles) + [
        pl.BlockSpec((None, None, nb, 3, d), mod_map), _wspec(w["norm_g"], (li,)),
        pl.BlockSpec((nb, tt, bw), lambda i, j: (i, jnp.minimum(j, lat_tiles - 1), 0))]
    args = [*toks, mod3, w["norm_g"], o_lat]
    if with_ctx:
        ctx_tiles = o_ctx.shape[1] // tt
        in_specs.append(pl.BlockSpec((nb, tt, bw), lambda i, j: (i, jnp.clip(j - lat_tiles, 0, ctx_tiles - 1), 0)))
        args.append(o_ctx)
    in_specs += [tm(2 * bw), tm(2 * bw), tm(bw), bm(bw)] + [_wspec(w[n], (li,)) for n in names]
    args += [yf, yb, u_tm, o_pool] + [w[n] for n in names]
    kern = functools.partial(_merge_kernel, bw=bw, lat_tiles=lat_tiles, with_ctx=with_ctx, n_tok=len(toks))
    return pl.pallas_call(
        kern,
        out_shape=jax.ShapeDtypeStruct((b, n_out, d), F32),
        grid=(b // nb, n_out // tt),
        in_specs=in_specs,
        out_specs=bm(d),
        compiler_params=_params(("arbitrary", "arbitrary")),
        name="merge_out",
    )(*args)


def _block_diag(blocks):
    n, a, b = blocks.shape[-3:]
    eye = jnp.eye(n, dtype=blocks.dtype)
    return jnp.einsum("...nab,nm->...namb", blocks, eye).reshape(blocks.shape[:-3] + (n * a, n * b))


def _rope_perm():
    half = ROPE // 2
    nf = half // 2
    base = jnp.concatenate([jnp.arange(nf, half), jnp.arange(0, nf)])
    return jnp.concatenate([base, base + half])


def _head_pad(a, lo):
    n = a.shape[-1]
    pad = [(0, 0)] * (a.ndim - 1) + [(lo, HEAD_PAD - lo - n)]
    return jnp.pad(a, pad)


def _rope_tables(n_lat, n_ctx):
    rows_n = n_lat // GRID_W
    row = jnp.repeat(jnp.arange(rows_n, dtype=jnp.int32), GRID_W).astype(F32)
    col = jnp.tile(jnp.arange(GRID_W, dtype=jnp.int32), rows_n).astype(F32)
    nf = ROPE // 4
    inv = ROPE_THETA ** (-jnp.arange(nf, dtype=F32) / nf)
    ang_r = row[:, None] * inv
    ang_c = col[:, None] * inv
    cr, sr, cc, sc = jnp.cos(ang_r), jnp.sin(ang_r), jnp.cos(ang_c), jnp.sin(ang_c)
    ones = jnp.ones((n_lat, NOPE), F32)
    zeros = jnp.zeros((n_lat, NOPE), F32)
    tail = HEAD_PAD - QK
    cos = jnp.concatenate([ones, cr, cr, cc, cc, jnp.ones((n_lat, tail), F32)], -1)
    sin = jnp.concatenate([zeros, -sr, sr, -sc, sc, jnp.zeros((n_lat, tail), F32)], -1)
    cos = jnp.concatenate([cos, jnp.ones((n_ctx, HEAD_PAD), F32)], 0)
    sin = jnp.concatenate([sin, jnp.zeros((n_ctx, HEAD_PAD), F32)], 0)
    return cos, sin


def _prep_weights(p, disc):
    w_in = p["w_in"]
    depth, d, _ = w_in.shape
    bw = d // N_BRANCH
    kv_lora = p["mla_kv_norm"].shape[-1]
    q_lora = p["mla_q_norm"].shape[-1]
    perm = _rope_perm()
    off_krope = kv_lora
    off_s5 = off_krope + ROPE
    off_lru = off_s5 + bw
    off_cq = off_lru + bw
    off_pool = off_cq + q_lora
    off_gate = off_pool + bw
    w_krope = w_in[:, :, off_krope:off_s5]
    w1 = jnp.concatenate([
        w_in[:, :, :kv_lora],
        _head_pad(w_krope, NOPE),
        _head_pad(w_krope[:, :, perm], NOPE),
        w_in[:, :, off_cq:off_pool],
        w_in[:, :, off_s5:off_lru],
        w_in[:, :, off_lru:off_cq],
        w_in[:, :, off_pool:off_gate]], axis=2).astype(BF16)
    w2 = w_in[:, :, off_gate:].astype(BF16)

    w_uq = p["mla_w_uq"].reshape(depth, q_lora, HEADS, QK)
    wq = _head_pad(w_uq, 0).reshape(depth, q_lora, HEADS * HEAD_PAD).astype(BF16)
    wqp = _head_pad(w_uq[..., NOPE:][..., perm], NOPE).reshape(depth, q_lora, HEADS * HEAD_PAD).astype(BF16)
    w_ukv = p["mla_w_ukv"].reshape(depth, kv_lora, HEADS, NOPE + VDIM)
    wk = _head_pad(w_ukv[..., :NOPE], 0).reshape(depth, kv_lora, HEADS * HEAD_PAD).astype(BF16)
    wv = w_ukv[..., NOPE:].reshape(depth, kv_lora, HEADS * VDIM).astype(BF16)
    q_gain, k_gain = p["mla_q_gain"], p["mla_k_gain"]
    row = lambda a: a[..., None, :]

    ab_re, ab_im, bb_re, bb_im = disc
    sw = lambda a: jnp.swapaxes(a, -1, -2)
    bmat = jnp.concatenate([_block_diag(sw(bb_re)), _block_diag(sw(bb_im))], axis=-1).astype(BF16)
    return dict(
        norm_g=row(p["norm_g"]), w1=w1, w2=w2,
        kv_norm=row(p["mla_kv_norm"]), q_norm=row(p["mla_q_norm"]),
        wq=wq, wqp=wqp, wk=wk, wv=wv,
        qgain=row(_head_pad(q_gain, 0)), qgainp=row(_head_pad(q_gain[:, NOPE:][:, perm], NOPE)),
        kgain=row(_head_pad(k_gain, 0)), kgainp=row(_head_pad(k_gain[:, NOPE:][:, perm], NOPE)),
        conv_w=p["lru_conv_w"], conv_b=row(p["lru_conv_b"]),
        pool_w=_block_diag(p["pool_w"]).astype(BF16), pool_b=row(p["pool_b"]),
        pool_scale=row(p["pool_scale"]),
        wb=p["w_branch"].astype(BF16), wo=p["w_out"].astype(BF16),
        wglu=p["s5_w_glu"].astype(BF16), s5_d=row(p["s5_d"]),
        bmat=bmat,
        cre=_block_diag(sw(p["s5_c_re"])).astype(BF16), cim=_block_diag(sw(p["s5_c_im"])).astype(BF16),
        ar=row(ab_re.reshape(ab_re.shape[:2] + (-1,))), ai=row(ab_im.reshape(ab_im.shape[:2] + (-1,))),
        wa=_block_diag(p["lru_w_a"]).astype(BF16), ba=row(p["lru_b_a"]),
        wx=_block_diag(p["lru_w_x"]).astype(BF16), bx=row(p["lru_b_x"]),
        lam=row(p["lru_lambda"]))


def kernel(x, c, ctx, c_ctx, w_ada, b_ada, norm_g, w_in, mla_q_norm, mla_kv_norm, mla_w_uq, mla_w_ukv, mla_q_gain, mla_k_gain, s5_a_re, s5_a_im, s5_log_dt, s5_b_re, s5_b_im, s5_c_re, s5_c_im, s5_d, s5_w_glu, lru_conv_w, lru_conv_b, lru_lambda, lru_w_a, lru_b_a, lru_w_x, lru_b_x, pool_w, pool_b, pool_scale, w_branch, w_out):
    p = dict(w_in=w_in, norm_g=norm_g, mla_q_norm=mla_q_norm, mla_kv_norm=mla_kv_norm,
             mla_w_uq=mla_w_uq, mla_w_ukv=mla_w_ukv, mla_q_gain=mla_q_gain, mla_k_gain=mla_k_gain,
             s5_c_re=s5_c_re, s5_c_im=s5_c_im, s5_d=s5_d, s5_w_glu=s5_w_glu,
             lru_conv_w=lru_conv_w, lru_conv_b=lru_conv_b, lru_lambda=lru_lambda,
             lru_w_a=lru_w_a, lru_b_a=lru_b_a, lru_w_x=lru_w_x, lru_b_x=lru_b_x,
             pool_w=pool_w, pool_b=pool_b, pool_scale=pool_scale, w_branch=w_branch, w_out=w_out)
    depth = w_in.shape[0]
    b, l, d = x.shape
    lc = ctx.shape[1]
    t = l + lc
    assert b % SUBLANE == 0 and l % PROJ_STEPS == 0 and lc % PROJ_STEPS == 0 and l % lc == 0

    n_rows = -(-(b + 1) // SUBLANE) * SUBLANE
    cc = jnp.concatenate([c, c_ctx[None], jnp.zeros((n_rows - b - 1, d), F32)], axis=0)
    mod_all = _ada(cc, w_ada, b_ada)
    mod3 = jnp.concatenate([
        mod_all[:, :b].reshape(depth, b // SUBLANE, SUBLANE, 3, d),
        jnp.broadcast_to(mod_all[:, b:b + 1].reshape(depth, 1, 1, 3, d), (depth, 1, SUBLANE, 3, d))], axis=1)
    disc = _s5_discretize(s5_a_re, s5_a_im, s5_log_dt, s5_b_re, s5_b_im)
    w = _prep_weights(p, disc)
    cos_t, sin_t = _rope_tables(l, lc)
    tq_l, tq_c = min(Q_TILE, l), min(Q_TILE, lc)

    toks = (x, ctx)
    for li in range(depth):
        with_ctx = li < depth - 1
        q, k, vt, u_tm, xp = _proj(toks, mod3, w, li, cos_t, sin_t, l)
        xl, o_pool = _seq(xp, w, li, ((0, l), (l, lc)))
        o_lat = _attend(q, k, vt, l, 0, t, 0, tq_l)
        o_ctx = _attend(q, k, vt, lc, l, lc, l, tq_c) if with_ctx else None
        yf, yb = _scan(u_tm, xl, w, li, l)
        toks = (_merge(toks, mod3, o_lat, o_ctx, yf, yb, u_tm, o_pool, w, li, l, t if with_ctx else l),)
    return toks[0]
```

```python
import functools
import math

import jax
import jax.numpy as jnp
from jax import lax
from jax.experimental import pallas as pl
from jax.experimental.pallas import tpu as pltpu

F32 = jnp.float32
BF16 = jnp.bfloat16

EPS = 1e-6
GRID_W = 64
N_BRANCH = 4
HEADS = 4
NOPE = 64
ROPE = 32
VDIM = 64
VAUG = 80
QK = NOPE + ROPE
ROPE_THETA = 10000.0
LRU_CONV = 4
LRU_C = 8.0
POOL_WINDOWS = (2, 4, 8, 16)
Q_SCALE = (QK ** -0.5) * math.log2(math.e)

LANE = 128
SUBLANE = 8
HEAD_PAD = LANE
VMEM_LIMIT = 56 * 1024 * 1024

PROJ_STEPS = 128
PROJ_ROW_SPLITS = 2
MERGE_STEPS = 64
MERGE_ROW_SPLITS = 2
Q_TILE = 512
KEY_CHUNK = 256
ATTN_LOOKAHEAD = 2
SCAN_CHUNK = 64
SCAN_SUB = 8
SCAN_LANE_SPLITS = 2
HALO = 8


def _dot(a, b):
    return jnp.dot(a, b, preferred_element_type=F32)


def _dot_nt(a, b):
    return lax.dot_general(a, b, (((1,), (1,)), ((), ())), preferred_element_type=F32)


def _rms(x, g):
    return x * lax.rsqrt(jnp.mean(x * x, axis=-1, keepdims=True) + EPS) * g


def _wspec(arr, lead):
    tail = arr.shape[len(lead):]
    idx = tuple(lead) + (0,) * len(tail)
    return pl.BlockSpec((None,) * len(lead) + tail, lambda *_: idx, pipeline_mode=pl.Buffered(1))


def _token_specs(toks, nb, tt, lat_tiles):
    if len(toks) == 1:
        return [pl.BlockSpec((nb, tt, toks[0].shape[2]), lambda i, j: (i, j, 0))]
    lat, ctx = toks
    ctx_tiles = ctx.shape[1] // tt
    return [pl.BlockSpec((nb, tt, lat.shape[2]), lambda i, j: (i, jnp.minimum(j, lat_tiles - 1), 0)),
            pl.BlockSpec((nb, tt, ctx.shape[2]), lambda i, j: (i, jnp.clip(j - lat_tiles, 0, ctx_tiles - 1), 0))]


def _read_tokens(tok_refs, lat_tiles):
    if len(tok_refs) == 1:
        return tok_refs[0][...]
    return jnp.where(pl.program_id(1) >= lat_tiles, tok_refs[1][...], tok_refs[0][...])


def _params(sem):
    return pltpu.CompilerParams(dimension_semantics=sem, vmem_limit_bytes=VMEM_LIMIT)


def _ada_kernel(c_ref, w_ref, b_ref, o_ref):
    c = c_ref[...]
    act = c * jax.nn.sigmoid(c)
    o_ref[0] = jnp.dot(act, w_ref[0], preferred_element_type=F32,
                       precision=lax.Precision.HIGHEST) + b_ref[0]


def _ada(cc, w_ada, b_ada):
    depth, d, d3 = w_ada.shape
    rows = cc.shape[0]
    col = d
    return pl.pallas_call(
        _ada_kernel,
        out_shape=jax.ShapeDtypeStruct((depth, rows, d3), F32),
        grid=(depth, d3 // col),
        in_specs=[pl.BlockSpec((rows, d), lambda l, j: (0, 0)),
                  pl.BlockSpec((1, d, col), lambda l, j: (l, 0, j)),
                  pl.BlockSpec((1, 1, col), lambda l, j: (l, 0, j))],
        out_specs=pl.BlockSpec((1, rows, col), lambda l, j: (l, 0, j)),
        compiler_params=_params(("arbitrary", "arbitrary")),
        name="ada_mod",
    )(cc, w_ada, b_ada.reshape(depth, 1, d3))


def _s5_disc_kernel(are_ref, aim_ref, ldt_ref, bre_ref, bim_ref, abre_ref, abim_ref, bbre_ref, bbim_ref):
    a_re = are_ref[...]
    a_im = aim_ref[...]
    dt = jnp.exp(ldt_ref[...])
    mag = jnp.exp(a_re * dt)
    ab_re = mag * jnp.cos(a_im * dt)
    ab_im = mag * jnp.sin(a_im * dt)
    den = a_re * a_re + a_im * a_im
    f_re = ((ab_re - 1.0) * a_re + ab_im * a_im) / den
    f_im = (ab_im * a_re - (ab_re - 1.0) * a_im) / den
    b_re = bre_ref[...]
    b_im = bim_ref[...]
    abre_ref[...] = ab_re
    abim_ref[...] = ab_im
    bbre_ref[...] = f_re * b_re - f_im * b_im
    bbim_ref[...] = f_re * b_im + f_im * b_re


def _s5_discretize(a_re, a_im, log_dt, b_re, b_im):
    full = b_re.shape
    rows = b_re.size // LANE
    expand = lambda a: jnp.broadcast_to(a[..., None], full).reshape(rows, LANE)
    ins = [expand(a_re), expand(a_im), expand(jnp.broadcast_to(log_dt[..., None], a_re.shape)),
           b_re.reshape(rows, LANE), b_im.reshape(rows, LANE)]
    outs = pl.pallas_call(
        _s5_disc_kernel,
        out_shape=[jax.ShapeDtypeStruct((rows, LANE), F32)] * 4,
        name="s5_discretize",
    )(*ins)
    ab_re, ab_im, bb_re, bb_im = [o.reshape(full) for o in outs]
    return ab_re[..., 0], ab_im[..., 0], bb_re, bb_im


def _proj_kernel(*refs, kv_lora, q_lora, bw, n_tok, lat_tiles):
    tok_refs = refs[:n_tok]
    (mod_ref, ng_ref, w1_ref, kvg_ref, wk_ref, wv_ref, qg_ref, wq_ref, wqp_ref,
     qgain_ref, qgainp_ref, kgain_ref, kgainp_ref, cos_ref, sin_ref,
     q_out, k_out, vt_out, u_out, xp_out) = refs[n_tok:]
    nb_all, tt, d = tok_refs[0].shape
    x_all = _read_tokens(tok_refs, lat_tiles)
    mod_all = mod_ref[...]
    nb = nb_all // PROJ_ROW_SPLITS
    rows = nb * tt
    cos = jnp.concatenate([cos_ref[...]] * nb, axis=0)
    sin = jnp.concatenate([sin_ref[...]] * nb, axis=0)
    inv_qk = 1.0 / QK
    qgain, qgainp = qgain_ref[...], qgainp_ref[...]
    kgain, kgainp = kgain_ref[...], kgainp_ref[...]

    staged = []
    for s in range(PROJ_ROW_SPLITS):
        bs = slice(s * nb, (s + 1) * nb)
        mod = mod_all[bs]
        shift, scale = mod[:, 0:1, :], mod[:, 1:2, :]
        h = _rms(x_all[bs], ng_ref[...]) * (1.0 + scale) + shift
        z = _dot(h.reshape(rows, d).astype(BF16), w1_ref[...])
        o = 0
        ckv = z[:, o:o + kv_lora]; o += kv_lora
        kr = z[:, o:o + HEAD_PAD]; o += HEAD_PAD
        kp = z[:, o:o + HEAD_PAD]; o += HEAD_PAD
        cq = z[:, o:o + q_lora]; o += q_lora
        u_out[:, bs, :] = jnp.transpose(z[:, o:o + bw].reshape(nb, tt, bw), (1, 0, 2)); o += bw
        xp_out[bs] = z[:, o:o + 2 * bw].reshape(nb, tt, 2 * bw)
        cqn = _rms(cq, qg_ref[...]).astype(BF16)
        ckvn = _rms(ckv, kvg_ref[...]).astype(BF16)
        staged.append((bs, kr, kp, _dot(cqn, wq_ref[...]), _dot(cqn, wqp_ref[...]),
                       _dot(ckvn, wk_ref[...]), _dot(ckvn, wv_ref[...])))

    for bs, kr, kp, qraw, qprt, knope, v in staged:
        for hh in range(HEADS):
            sl = slice(hh * HEAD_PAD, (hh + 1) * HEAD_PAD)
            qh = qraw[:, sl]
            rs = lax.rsqrt(jnp.sum(qh * qh, axis=-1, keepdims=True) * inv_qk + EPS)
            qn = qh * rs * qgain
            qpn = qprt[:, sl] * rs * qgainp
            q_out[bs, hh] = ((qn * cos + qpn * sin) * Q_SCALE).astype(BF16).reshape(nb, tt, HEAD_PAD)
        for hh in range(HEADS):
            sl = slice(hh * HEAD_PAD, (hh + 1) * HEAD_PAD)
            kh = knope[:, sl] + kr
            rs = lax.rsqrt(jnp.sum(kh * kh, axis=-1, keepdims=True) * inv_qk + EPS)
            kn = kh * rs * kgain
            kpn = kp * rs * kgainp
            k_out[bs, hh] = (kn * cos + kpn * sin).astype(BF16).reshape(nb, tt, HEAD_PAD)
        ones_rows = (lax.broadcasted_iota(jnp.int32, (HEADS, VAUG - VDIM, tt), 1) == 0).astype(BF16)
        for bi in range(nb):
            vt = v[bi * tt:(bi + 1) * tt].T.astype(BF16).reshape(HEADS, VDIM, tt)
            vt_out[bs.start + bi] = jnp.concatenate([vt, ones_rows], axis=1)


def _proj(toks, mod3, w, li, cos_t, sin_t, n_lat):
    b, _, d = toks[0].shape
    t = sum(a.shape[1] for a in toks)
    nb, tt = SUBLANE, PROJ_STEPS
    bw = d // N_BRANCH
    kv_lora, q_lora = w["kv_norm"].shape[-1], w["q_norm"].shape[-1]
    ctx_group = mod3.shape[1] - 1
    lat_tiles = n_lat // tt
    mod_map = lambda i, j: (li, jnp.where(j < lat_tiles, i, ctx_group), 0, 0, 0)
    kern = functools.partial(_proj_kernel, kv_lora=kv_lora, q_lora=q_lora, bw=bw, n_tok=len(toks),
                             lat_tiles=lat_tiles)
    names = ["norm_g", "w1", "kv_norm", "wk", "wv", "q_norm", "wq", "wqp", "qgain", "qgainp", "kgain", "kgainp"]
    return pl.pallas_call(
        kern,
        out_shape=[jax.ShapeDtypeStruct((b, HEADS, t, HEAD_PAD), BF16),
                   jax.ShapeDtypeStruct((b, HEADS, t, HEAD_PAD), BF16),
                   jax.ShapeDtypeStruct((b, HEADS, VAUG, t), BF16),
                   jax.ShapeDtypeStruct((t, b, bw), F32),
                   jax.ShapeDtypeStruct((b, t, 2 * bw), F32)],
        grid=(b // nb, t // tt),
        in_specs=_token_specs(toks, nb, tt, lat_tiles)
                 + [pl.BlockSpec((None, None, nb, 3, d), mod_map)]
                 + [_wspec(w[n], (li,)) for n in names]
                 + [pl.BlockSpec((tt, HEAD_PAD), lambda i, j: (j, 0)),
                    pl.BlockSpec((tt, HEAD_PAD), lambda i, j: (j, 0))],
        out_specs=[pl.BlockSpec((nb, HEADS, tt, HEAD_PAD), lambda i, j: (i, 0, j, 0)),
                   pl.BlockSpec((nb, HEADS, tt, HEAD_PAD), lambda i, j: (i, 0, j, 0)),
                   pl.BlockSpec((nb, HEADS, VAUG, tt), lambda i, j: (i, 0, 0, j)),
                   pl.BlockSpec((tt, nb, bw), lambda i, j: (j, i, 0)),
                   pl.BlockSpec((nb, tt, 2 * bw), lambda i, j: (i, j, 0))],
        compiler_params=_params(("arbitrary", "arbitrary")),
        name="proj_qkv",
    )(*toks, mod3, *[w[n] for n in names], cos_t, sin_t)


def _seq_kernel(xp_ref, cw_ref, cb_ref, pw_ref, pb_ref, ps_ref, xl_out, op_out,
                pad_ref, f2_ref, f4_ref, f8_ref, inv_ref, *, segs, bw):
    assert POOL_WINDOWS == (2, 4, 8, 16) and bw == 2 * LANE and HALO == 8
    gw = bw // len(POOL_WINDOWS)
    cw = cw_ref[...]
    left = LRU_CONV // 2

    @pl.when(pl.program_id(0) == 0)
    def _():
        for lo, t in segs:
            lane = lax.broadcasted_iota(jnp.int32, (t, bw), 1)
            row = lax.broadcasted_iota(jnp.int32, (t, bw), 0)
            half = jnp.ones((t, bw), jnp.int32)
            for gi in range(1, len(POOL_WINDOWS)):
                half = jnp.where(lane >= gi * gw, POOL_WINDOWS[gi] // 2, half)
            cnt = (jnp.minimum(row + half, t) - jnp.maximum(row - half, 0)).astype(F32)
            inv_ref[lo:lo + t, :] = 1.0 / cnt

    lane_half = lax.broadcasted_iota(jnp.int32, (1, LANE), 1) < gw
    for lo, t in segs:
        n = t + 2 * HALO
        pad_ref[0:HALO, :] = jnp.zeros((HALO, 2 * bw), F32)
        pad_ref[HALO + t:n + HALO, :] = jnp.zeros((2 * HALO, 2 * bw), F32)
        pad_ref[HALO:HALO + t, :] = xp_ref[0, lo:lo + t, :]
        f2_ref[n:n + HALO, :] = jnp.zeros((HALO, bw), F32)
        f4_ref[n:n + HALO, :] = jnp.zeros((HALO, bw), F32)
        f8_ref[n:n + HALO, :] = jnp.zeros((HALO, LANE), F32)

        acc = cb_ref[...] + pad_ref[HALO - left:HALO - left + t, 0:bw] * cw[0:1]
        for k in range(1, LRU_CONV):
            acc = acc + pad_ref[HALO + k - left:HALO + k - left + t, 0:bw] * cw[k:k + 1]
        xl_out[0, lo:lo + t, :] = acc

        f2_ref[0:n, :] = pad_ref[0:n, bw:2 * bw] + pad_ref[1:n + 1, bw:2 * bw]
        f4_ref[0:n, :] = f2_ref[0:n, :] + f2_ref[2:n + 2, :]
        f8_ref[0:n, :] = f4_ref[0:n, LANE:bw] + f4_ref[4:n + 4, LANE:bw]
        sel_lo = jnp.where(lane_half, f2_ref[7:7 + t, 0:LANE], f4_ref[6:6 + t, 0:LANE])
        sel_hi = jnp.where(lane_half, f8_ref[4:4 + t, :], f8_ref[0:t, :] + f8_ref[HALO:HALO + t, :])
        sel = jnp.concatenate([sel_lo, sel_hi], axis=1)
        p = sel * inv_ref[lo:lo + t, :] - pad_ref[HALO:HALO + t, bw:2 * bw]
        y = _dot(p.astype(BF16), pw_ref[...]) + pb_ref[...]
        op_out[0, lo:lo + t, :] = y * ps_ref[...]


def _seq(xp, w, li, segs):
    b, t, w2 = xp.shape
    bw = w2 // 2
    kern = functools.partial(_seq_kernel, segs=segs, bw=bw)
    names = ["conv_w", "conv_b", "pool_w", "pool_b", "pool_scale"]
    max_t = max(n for _, n in segs)
    return pl.pallas_call(
        kern,
        out_shape=[jax.ShapeDtypeStruct((b, t, bw), F32), jax.ShapeDtypeStruct((b, t, bw), F32)],
        grid=(b,),
        in_specs=[pl.BlockSpec((1, t, w2), lambda i: (i, 0, 0))] + [_wspec(w[n], (li,)) for n in names],
        out_specs=[pl.BlockSpec((1, t, bw), lambda i: (i, 0, 0)),
                   pl.BlockSpec((1, t, bw), lambda i: (i, 0, 0))],
        scratch_shapes=[pltpu.VMEM((max_t + 3 * HALO, w2), F32), pltpu.VMEM((max_t + 3 * HALO, bw), F32),
                        pltpu.VMEM((max_t + 3 * HALO, bw), F32), pltpu.VMEM((max_t + 3 * HALO, LANE), F32),
                        pltpu.VMEM((t, bw), F32)],
        compiler_params=_params(("arbitrary",)),
        name="conv_pool",
    )(xp, *[w[n] for n in names])


def _attn_kernel(q_ref, k_ref, vt_ref, o_ref):
    n_k = k_ref.shape[2]
    ck = min(KEY_CHUNK, n_k)
    nch = n_k // ck
    items = [(hh, c) for hh in range(HEADS) for c in range(nch)]

    def scores(item):
        hh, c = item
        return _dot_nt(k_ref[0, hh, c * ck:(c + 1) * ck, :], q_ref[0, hh])

    outs = []
    m = acc = None
    ahead = [scores(it) for it in items[:ATTN_LOOKAHEAD]]
    for i, (hh, c) in enumerate(items):
        sc = ahead.pop(0)
        if i + ATTN_LOOKAHEAD < len(items):
            ahead.append(scores(items[i + ATTN_LOOKAHEAD]))
        ks = slice(c * ck, (c + 1) * ck)
        mc = jnp.max(sc, axis=0, keepdims=True)
        if c == 0:
            m_new = mc
            acc = _dot(vt_ref[0, hh, :, ks], jnp.exp2(sc - m_new).astype(BF16))
        else:
            m_new = jnp.maximum(m, mc)
            acc = jnp.exp2(m - m_new) * acc + _dot(vt_ref[0, hh, :, ks], jnp.exp2(sc - m_new).astype(BF16))
        m = m_new
        if c == nch - 1:
            outs.append(acc[0:VDIM] / acc[VDIM:VDIM + 1])
    o_ref[0] = jnp.concatenate(outs, axis=0).T


def _attend(q, k, vt, n_q, q_off, n_k, k_off, tq):
    b = q.shape[0]
    hv = HEADS * VDIM
    assert n_q % tq == 0 and q_off % tq == 0 and k_off % n_k == 0
    qo, ko = q_off // tq, k_off // n_k
    return pl.pallas_call(
        _attn_kernel,
        out_shape=jax.ShapeDtypeStruct((b, n_q, hv), F32),
        grid=(b, n_q // tq),
        in_specs=[pl.BlockSpec((1, HEADS, tq, HEAD_PAD), lambda i, j: (i, 0, j + qo, 0)),
                  pl.BlockSpec((1, HEADS, n_k, HEAD_PAD), lambda i, j: (i, 0, ko, 0)),
                  pl.BlockSpec((1, HEADS, VAUG, n_k), lambda i, j: (i, 0, 0, ko))],
        out_specs=pl.BlockSpec((1, tq, hv), lambda i, j: (i, j, 0)),
        compiler_params=_params(("arbitrary", "arbitrary")),
        name="attention",
    )(q, k, vt)


def _scan_kernel(uf_ref, ub_ref, xf_ref, xb_ref, bmat_ref, cre_ref, cim_ref, ar_ref, ai_ref,
                 wa_ref, ba_ref, wx_ref, bx_ref, lam_ref, yf_ref, yb_ref,
                 hbuf, abuf, bbuf, hr_s, hi_s, hl_s, *, tc, bsz, bw, ns):
    r = tc * bsz
    sb = SCAN_SUB
    nsub = tc // sb
    rs = sb * bsz
    nblk = bw // LANE
    sblk = ns // nblk
    u_refs, x_refs, y_refs = (uf_ref, ub_ref), (xf_ref, xb_ref), (yf_ref, yb_ref)

    @pl.when(pl.program_id(0) == 0)
    def _():
        hr_s[...] = jnp.zeros_like(hr_s)
        hi_s[...] = jnp.zeros_like(hi_s)
        hl_s[...] = jnp.zeros_like(hl_s)

    u16 = [u_refs[d][...].reshape(r, bw).astype(BF16) for d in range(2)]
    xls = [jnp.transpose(x_refs[d][...], (1, 0, 2)).reshape(r, bw) for d in range(2)]
    softplus = []
    for d in range(2):
        nl = -lam_ref[d]
        softplus.append(jnp.maximum(nl, 0.0) + jnp.log1p(jnp.exp(-jnp.abs(nl))))

    def sub_rows(d, k):
        kk = (nsub - 1 - k) if d else k
        return kk, slice(kk * rs, (kk + 1) * rs)

    def input_stage(k):
        for d in range(2):
            _, rw = sub_rows(d, k)
            hbuf[d, rw, :] = _dot(u16[d][rw], bmat_ref[d])
            xl = xls[d][rw]
            xb = xl.astype(BF16)
            rg = jax.nn.sigmoid(_dot(xb, wa_ref[d]) + ba_ref[d])
            ig = jax.nn.sigmoid(_dot(xb, wx_ref[d]) + bx_ref[d])
            log_a = (-LRU_C) * rg * softplus[d]
            abuf[d, rw, :] = jnp.exp(log_a)
            th = jnp.tanh(log_a)
            gain2 = -2.0 * th / (1.0 - th)
            gain = jnp.where(gain2 > 0.0, gain2 * lax.rsqrt(gain2), 0.0)
            bbuf[d, rw, :] = gain * (ig * xl)

    w = ns // SCAN_LANE_SPLITS
    lanes = [(slice(j * w, (j + 1) * w), slice(ns + j * w, ns + (j + 1) * w)) for j in range(SCAN_LANE_SPLITS)]
    coef = [[(jnp.broadcast_to(ar_ref[d, :, re_sl], (bsz, w)), jnp.broadcast_to(ai_ref[d, :, re_sl], (bsz, w)))
             for re_sl, _ in lanes] for d in range(2)]
    hr = [[hr_s[d, :, re_sl] for re_sl, _ in lanes] for d in range(2)]
    hi = [[hi_s[d, :, re_sl] for re_sl, _ in lanes] for d in range(2)]
    hl = [hl_s[d] for d in range(2)]

    def scan_stage(k):
        def step_rows(t, d):
            te = (tc - 1 - t) if d else t
            return te, slice(te * bsz, (te + 1) * bsz)

        for t in range(k * sb, (k + 1) * sb):
            for d in range(2):
                te, rw = step_rows(t, d)
                hl[d] = abuf[d, rw, :] * hl[d] + bbuf[d, rw, :]
                y_refs[d][te, :, bw:2 * bw] = hl[d]
        for j, (re_sl, im_sl) in enumerate(lanes):
            for t in range(k * sb, (k + 1) * sb):
                for d in range(2):
                    _, rw = step_rows(t, d)
                    ar, ai = coef[d][j]
                    nhr = ar * hr[d][j] - ai * hi[d][j] + hbuf[d, rw, re_sl]
                    nhi = ar * hi[d][j] + ai * hr[d][j] + hbuf[d, rw, im_sl]
                    hbuf[d, rw, re_sl] = nhr
                    hbuf[d, rw, im_sl] = nhi
                    hr[d][j], hi[d][j] = nhr, nhi

    def readout_stage(k):
        for d in range(2):
            kk, rw = sub_rows(d, k)
            for jb in range(nblk):
                st = slice(jb * sblk, (jb + 1) * sblk)
                sti = slice(ns + jb * sblk, ns + (jb + 1) * sblk)
                ln = slice(jb * LANE, (jb + 1) * LANE)
                y = (_dot(hbuf[d, rw, st].astype(BF16), cre_ref[d, st, ln])
                     - _dot(hbuf[d, rw, sti].astype(BF16), cim_ref[d, st, ln]))
                y_refs[d][kk * sb:(kk + 1) * sb, :, ln] = y.reshape(sb, bsz, LANE)

    lead = min(2, nsub)
    for k in range(lead):
        input_stage(k)
    for k in range(nsub):
        if k + lead < nsub:
            input_stage(k + lead)
        scan_stage(k)
        if k >= 1:
            readout_stage(k - 1)
    readout_stage(nsub - 1)

    for d in range(2):
        hl_s[d] = hl[d]
        for j, (re_sl, _) in enumerate(lanes):
            hr_s[d, :, re_sl] = hr[d][j]
            hi_s[d, :, re_sl] = hi[d][j]


def _scan(u_tm, xl, w, li, n_lat):
    t, bsz, bw = u_tm.shape
    tc = SCAN_CHUNK
    nc = t // tc
    ncl = n_lat // tc
    ncc = nc - ncl
    ns = w["ar"].shape[-1]
    fwd = lambda s: jnp.where(s < ncc, ncl + s, s - ncc)
    bwd = lambda s: nc - 1 - s
    names = ["bmat", "cre", "cim", "ar", "ai", "wa", "ba", "wx", "bx", "lam"]
    r = tc * bsz
    kern = functools.partial(_scan_kernel, tc=tc, bsz=bsz, bw=bw, ns=ns)
    return pl.pallas_call(
        kern,
        out_shape=[jax.ShapeDtypeStruct((t, bsz, 2 * bw), F32)] * 2,
        grid=(nc,),
        in_specs=[pl.BlockSpec((tc, bsz, bw), lambda s: (fwd(s), 0, 0)),
                  pl.BlockSpec((tc, bsz, bw), lambda s: (bwd(s), 0, 0)),
                  pl.BlockSpec((bsz, tc, bw), lambda s: (0, fwd(s), 0)),
                  pl.BlockSpec((bsz, tc, bw), lambda s: (0, bwd(s), 0))]
                 + [_wspec(w[n], (li,)) for n in names],
        out_specs=[pl.BlockSpec((tc, bsz, 2 * bw), lambda s: (fwd(s), 0, 0)),
                   pl.BlockSpec((tc, bsz, 2 * bw), lambda s: (bwd(s), 0, 0))],
        scratch_shapes=[pltpu.VMEM((2, r, 2 * ns), F32), pltpu.VMEM((2, r, bw), F32), pltpu.VMEM((2, r, bw), F32),
                        pltpu.VMEM((2, bsz, ns), F32), pltpu.VMEM((2, bsz, ns), F32),
                        pltpu.VMEM((2, bsz, bw), F32)],
        compiler_params=_params(("arbitrary",)),
        name="scan_bidir",
    )(u_tm, u_tm, xl, xl, *[w[n] for n in names])


def _merge_kernel(*refs, bw, lat_tiles, with_ctx, n_tok):
    tok_refs, refs = refs[:n_tok], refs[n_tok:]
    if with_ctx:
        (mod_ref, ng_ref, om_ref, omc_ref, yf_ref, yb_ref, u_ref, op_ref,
         w2_ref, wb_ref, wo_ref, wglu_ref, d_ref, out_ref) = refs
    else:
        (mod_ref, ng_ref, om_ref, yf_ref, yb_ref, u_ref, op_ref,
         w2_ref, wb_ref, wo_ref, wglu_ref, d_ref, out_ref) = refs
    nb_all, tt, d = tok_refs[0].shape
    x_all = _read_tokens(tok_refs, lat_tiles)
    mod_all = mod_ref[...]
    rr_all = jnp.transpose(yf_ref[...] + yb_ref[...], (1, 0, 2))
    u_all = jnp.transpose(u_ref[...], (1, 0, 2))
    o_mla_all = om_ref[...]
    if with_ctx:
        o_mla_all = jnp.where(pl.program_id(1) >= lat_tiles, omc_ref[...], o_mla_all)
    op_all = op_ref[...]

    nb = nb_all // MERGE_ROW_SPLITS
    rows = nb * tt
    xs, hs, gates, branches = [], [], [], []
    for s in range(MERGE_ROW_SPLITS):
        bs = slice(s * nb, (s + 1) * nb)
        x = x_all[bs]
        mod = mod_all[bs]
        shift, scale, gate = mod[:, 0:1, :], mod[:, 1:2, :], mod[:, 2:3, :]
        h = (_rms(x, ng_ref[...]) * (1.0 + scale) + shift).reshape(rows, d).astype(BF16)
        rr = rr_all[bs].reshape(rows, 2 * bw)
        u = u_all[bs].reshape(rows, bw)
        g = jax.nn.gelu(rr[:, 0:bw] + d_ref[...] * u, approximate=True)
        o_s5 = g * jax.nn.sigmoid(_dot(g.astype(BF16), wglu_ref[...]))
        xs.append(x)
        hs.append(h)
        gates.append(gate)
        branches.append((o_mla_all[bs].reshape(rows, bw), o_s5, rr[:, bw:2 * bw], op_all[bs].reshape(rows, bw)))

    ys = [None] * MERGE_ROW_SPLITS
    for n in range(N_BRANCH):
        for s in range(MERGE_ROW_SPLITS):
            gp = _dot(hs[s], w2_ref[:, n * bw:(n + 1) * bw])
            off = N_BRANCH * bw + n * d
            ml = _dot(hs[s], w2_ref[:, off:off + d])
            tn = _dot((branches[s][n] * (gp * jax.nn.sigmoid(gp))).astype(BF16), wb_ref[n])
            term = jax.nn.sigmoid(ml) * tn
            ys[s] = term if ys[s] is None else ys[s] + term
    for s in range(MERGE_ROW_SPLITS):
        bs = slice(s * nb, (s + 1) * nb)
        out_ref[bs] = xs[s] + gates[s] * _dot(ys[s].astype(BF16), wo_ref[...]).reshape(nb, tt, d)


def _merge(toks, mod3, o_lat, o_ctx, yf, yb, u_tm, o_pool, w, li, n_lat, n_out):
    b, _, d = toks[0].shape
    nb, tt = SUBLANE, MERGE_STEPS
    bw = d // N_BRANCH
    with_ctx = o_ctx is not None
    lat_tiles = n_lat // tt
    ctx_group = mod3.shape[1] - 1
    mod_map = lambda i, j: (li, jnp.where(j < lat_tiles, i, ctx_group), 0, 0, 0)
    bm = lambda width: pl.BlockSpec((nb, tt, width), lambda i, j: (i, j, 0))
    tm = lambda width: pl.BlockSpec((tt, nb, width), lambda i, j: (j, i, 0))
    names = ["w2", "wb", "wo", "wglu", "s5_d"]
    in_specs = _token_specs(toks, nb, tt, lat_tiles) + [
        pl.BlockSpec((None, None, nb, 3, d), mod_map), _wspec(w["norm_g"], (li,)),
        pl.BlockSpec((nb, tt, bw), lambda i, j: (i, jnp.minimum(j, lat_tiles - 1), 0))]
    args = [*toks, mod3, w["norm_g"], o_lat]
    if with_ctx:
        ctx_tiles = o_ctx.shape[1] // tt
        in_specs.append(pl.BlockSpec((nb, tt, bw), lambda i, j: (i, jnp.clip(j - lat_tiles, 0, ctx_tiles - 1), 0)))
        args.append(o_ctx)
    in_specs += [tm(2 * bw), tm(2 * bw), tm(bw), bm(bw)] + [_wspec(w[n], (li,)) for n in names]
    args += [yf, yb, u_tm, o_pool] + [w[n] for n in names]
    kern = functools.partial(_merge_kernel, bw=bw, lat_tiles=lat_tiles, with_ctx=with_ctx, n_tok=len(toks))
    return pl.pallas_call(
        kern,
        out_shape=jax.ShapeDtypeStruct((b, n_out, d), F32),
        grid=(b // nb, n_out // tt),
        in_specs=in_specs,
        out_specs=bm(d),
        compiler_params=_params(("arbitrary", "arbitrary")),
        name="merge_out",
    )(*args)


def _block_diag(blocks):
    n, a, b = blocks.shape[-3:]
    eye = jnp.eye(n, dtype=blocks.dtype)
    return jnp.einsum("...nab,nm->...namb", blocks, eye).reshape(blocks.shape[:-3] + (n * a, n * b))


def _rope_perm():
    half = ROPE // 2
    nf = half // 2
    base = jnp.concatenate([jnp.arange(nf, half), jnp.arange(0, nf)])
    return jnp.concatenate([base, base + half])


def _head_pad(a, lo):
    n = a.shape[-1]
    pad = [(0, 0)] * (a.ndim - 1) + [(lo, HEAD_PAD - lo - n)]
    return jnp.pad(a, pad)


def _rope_tables(n_lat, n_ctx):
    rows_n = n_lat // GRID_W
    row = jnp.repeat(jnp.arange(rows_n, dtype=jnp.int32), GRID_W).astype(F32)
    col = jnp.tile(jnp.arange(GRID_W, dtype=jnp.int32), rows_n).astype(F32)
    nf = ROPE // 4
    inv = ROPE_THETA ** (-jnp.arange(nf, dtype=F32) / nf)
    ang_r = row[:, None] * inv
    ang_c = col[:, None] * inv
    cr, sr, cc, sc = jnp.cos(ang_r), jnp.sin(ang_r), jnp.cos(ang_c), jnp.sin(ang_c)
    ones = jnp.ones((n_lat, NOPE), F32)
    zeros = jnp.zeros((n_lat, NOPE), F32)
    tail = HEAD_PAD - QK
    cos = jnp.concatenate([ones, cr, cr, cc, cc, jnp.ones((n_lat, tail), F32)], -1)
    sin = jnp.concatenate([zeros, -sr, sr, -sc, sc, jnp.zeros((n_lat, tail), F32)], -1)
    cos = jnp.concatenate([cos, jnp.ones((n_ctx, HEAD_PAD), F32)], 0)
    sin = jnp.concatenate([sin, jnp.zeros((n_ctx, HEAD_PAD), F32)], 0)
    return cos, sin


def _prep_weights(p, disc):
    w_in = p["w_in"]
    depth, d, _ = w_in.shape
    bw = d // N_BRANCH
    kv_lora = p["mla_kv_norm"].shape[-1]
    q_lora = p["mla_q_norm"].shape[-1]
    perm = _rope_perm()
    off_krope = kv_lora
    off_s5 = off_krope + ROPE
    off_lru = off_s5 + bw
    off_cq = off_lru + bw
    off_pool = off_cq + q_lora
    off_gate = off_pool + bw
    w_krope = w_in[:, :, off_krope:off_s5]
    w1 = jnp.concatenate([
        w_in[:, :, :kv_lora],
        _head_pad(w_krope, NOPE),
        _head_pad(w_krope[:, :, perm], NOPE),
        w_in[:, :, off_cq:off_pool],
        w_in[:, :, off_s5:off_lru],
        w_in[:, :, off_lru:off_cq],
        w_in[:, :, off_pool:off_gate]], axis=2).astype(BF16)
    w2 = w_in[:, :, off_gate:].astype(BF16)

    w_uq = p["mla_w_uq"].reshape(depth, q_lora, HEADS, QK)
    wq = _head_pad(w_uq, 0).reshape(depth, q_lora, HEADS * HEAD_PAD).astype(BF16)
    wqp = _head_pad(w_uq[..., NOPE:][..., perm], NOPE).reshape(depth, q_lora, HEADS * HEAD_PAD).astype(BF16)
    w_ukv = p["mla_w_ukv"].reshape(depth, kv_lora, HEADS, NOPE + VDIM)
    wk = _head_pad(w_ukv[..., :NOPE], 0).reshape(depth, kv_lora, HEADS * HEAD_PAD).astype(BF16)
    wv = w_ukv[..., NOPE:].reshape(depth, kv_lora, HEADS * VDIM).astype(BF16)
    q_gain, k_gain = p["mla_q_gain"], p["mla_k_gain"]
    row = lambda a: a[..., None, :]

    ab_re, ab_im, bb_re, bb_im = disc
    sw = lambda a: jnp.swapaxes(a, -1, -2)
    bmat = jnp.concatenate([_block_diag(sw(bb_re)), _block_diag(sw(bb_im))], axis=-1).astype(BF16)
    return dict(
        norm_g=row(p["norm_g"]), w1=w1, w2=w2,
        kv_norm=row(p["mla_kv_norm"]), q_norm=row(p["mla_q_norm"]),
        wq=wq, wqp=wqp, wk=wk, wv=wv,
        qgain=row(_head_pad(q_gain, 0)), qgainp=row(_head_pad(q_gain[:, NOPE:][:, perm], NOPE)),
        kgain=row(_head_pad(k_gain, 0)), kgainp=row(_head_pad(k_gain[:, NOPE:][:, perm], NOPE)),
        conv_w=p["lru_conv_w"], conv_b=row(p["lru_conv_b"]),
        pool_w=_block_diag(p["pool_w"]).astype(BF16), pool_b=row(p["pool_b"]),
        pool_scale=row(p["pool_scale"]),
        wb=p["w_branch"].astype(BF16), wo=p["w_out"].astype(BF16),
        wglu=p["s5_w_glu"].astype(BF16), s5_d=row(p["s5_d"]),
        bmat=bmat,
        cre=_block_diag(sw(p["s5_c_re"])).astype(BF16), cim=_block_diag(sw(p["s5_c_im"])).astype(BF16),
        ar=row(ab_re.reshape(ab_re.shape[:2] + (-1,))), ai=row(ab_im.reshape(ab_im.shape[:2] + (-1,))),
        wa=_block_diag(p["lru_w_a"]).astype(BF16), ba=row(p["lru_b_a"]),
        wx=_block_diag(p["lru_w_x"]).astype(BF16), bx=row(p["lru_b_x"]),
        lam=row(p["lru_lambda"]))


def kernel(x, c, ctx, c_ctx, w_ada, b_ada, norm_g, w_in, mla_q_norm, mla_kv_norm, mla_w_uq, mla_w_ukv, mla_q_gain, mla_k_gain, s5_a_re, s5_a_im, s5_log_dt, s5_b_re, s5_b_im, s5_c_re, s5_c_im, s5_d, s5_w_glu, lru_conv_w, lru_conv_b, lru_lambda, lru_w_a, lru_b_a, lru_w_x, lru_b_x, pool_w, pool_b, pool_scale, w_branch, w_out):
    p = dict(w_in=w_in, norm_g=norm_g, mla_q_norm=mla_q_norm, mla_kv_norm=mla_kv_norm,
             mla_w_uq=mla_w_uq, mla_w_ukv=mla_w_ukv, mla_q_gain=mla_q_gain, mla_k_gain=mla_k_gain,
             s5_c_re=s5_c_re, s5_c_im=s5_c_im, s5_d=s5_d, s5_w_glu=s5_w_glu,
             lru_conv_w=lru_conv_w, lru_conv_b=lru_conv_b, lru_lambda=lru_lambda,
             lru_w_a=lru_w_a, lru_b_a=lru_b_a, lru_w_x=lru_w_x, lru_b_x=lru_b_x,
             pool_w=pool_w, pool_b=pool_b, pool_scale=pool_scale, w_branch=w_branch, w_out=w_out)
    depth = w_in.shape[0]
    b, l, d = x.shape
    lc = ctx.shape[1]
    t = l + lc
    assert b % SUBLANE == 0 and l % PROJ_STEPS == 0 and lc % PROJ_STEPS == 0 and l % lc == 0

    n_rows = -(-(b + 1) // SUBLANE) * SUBLANE
    cc = jnp.concatenate([c, c_ctx[None], jnp.zeros((n_rows - b - 1, d), F32)], axis=0)
    mod_all = _ada(cc, w_ada, b_ada)
    mod3 = jnp.concatenate([
        mod_all[:, :b].reshape(depth, b // SUBLANE, SUBLANE, 3, d),
        jnp.broadcast_to(mod_all[:, b:b + 1].reshape(depth, 1, 1, 3, d), (depth, 1, SUBLANE, 3, d))], axis=1)
    disc = _s5_discretize(s5_a_re, s5_a_im, s5_log_dt, s5_b_re, s5_b_im)
    w = _prep_weights(p, disc)
    cos_t, sin_t = _rope_tables(l, lc)
    tq_l, tq_c = min(Q_TILE, l), min(Q_TILE, lc)

    toks = (x, ctx)
    for li in range(depth):
        with_ctx = li < depth - 1
        q, k, vt, u_tm, xp = _proj(toks, mod3, w, li, cos_t, sin_t, l)
        xl, o_pool = _seq(xp, w, li, ((0, l), (l, lc)))
        o_lat = _attend(q, k, vt, l, 0, t, 0, tq_l)
        o_ctx = _attend(q, k, vt, lc, l, lc, l, tq_c) if with_ctx else None
        yf, yb = _scan(u_tm, xl, w, li, l)
        toks = (_merge(toks, mod3, o_lat, o_ctx, yf, yb, u_tm, o_pool, w, li, l, t if with_ctx else l),)
    return toks[0]
```

```python
import functools
import math

import jax
import jax.numpy as jnp
from jax import lax
from jax.experimental import pallas as pl
from jax.experimental.pallas import tpu as pltpu

F32 = jnp.float32
BF16 = jnp.bfloat16

EPS = 1e-6
GRID_W = 64
N_BRANCH = 4
HEADS = 4
NOPE = 64
ROPE = 32
VDIM = 64
VAUG = 80
QK = NOPE + ROPE
ROPE_THETA = 10000.0
LRU_CONV = 4
LRU_C = 8.0
POOL_WINDOWS = (2, 4, 8, 16)
Q_SCALE = (QK ** -0.5) * math.log2(math.e)

LANE = 128
SUBLANE = 8
HEAD_PAD = LANE
VMEM_LIMIT = 56 * 1024 * 1024

PROJ_STEPS = 128
PROJ_ROW_SPLITS = 2
MERGE_STEPS = 64
MERGE_ROW_SPLITS = 2
Q_TILE = 512
KEY_CHUNK = 256
ATTN_LOOKAHEAD = 2
SCAN_CHUNK = 64
SCAN_SUB = 8
SCAN_LANE_SPLITS = 2
HALO = 8


def _dot(a, b):
    return jnp.dot(a, b, preferred_element_type=F32)


def _dot_nt(a, b):
    return lax.dot_general(a, b, (((1,), (1,)), ((), ())), preferred_element_type=F32)


def _rms(x, g):
    return x * lax.rsqrt(jnp.mean(x * x, axis=-1, keepdims=True) + EPS) * g


def _wspec(arr, lead):
    tail = arr.shape[len(lead):]
    idx = tuple(lead) + (0,) * len(tail)
    return pl.BlockSpec((None,) * len(lead) + tail, lambda *_: idx, pipeline_mode=pl.Buffered(1))


def _token_specs(toks, nb, tt, lat_tiles):
    if len(toks) == 1:
        return [pl.BlockSpec((nb, tt, toks[0].shape[2]), lambda i, j: (i, j, 0))]
    lat, ctx = toks
    ctx_tiles = ctx.shape[1] // tt
    return [pl.BlockSpec((nb, tt, lat.shape[2]), lambda i, j: (i, jnp.minimum(j, lat_tiles - 1), 0)),
            pl.BlockSpec((nb, tt, ctx.shape[2]), lambda i, j: (i, jnp.clip(j - lat_tiles, 0, ctx_tiles - 1), 0))]


def _read_tokens(tok_refs, lat_tiles):
    if len(tok_refs) == 1:
        return tok_refs[0][...]
    return jnp.where(pl.program_id(1) >= lat_tiles, tok_refs[1][...], tok_refs[0][...])


def _params(sem):
    return pltpu.CompilerParams(dimension_semantics=sem, vmem_limit_bytes=VMEM_LIMIT)


def _ada_kernel(c_ref, w_ref, b_ref, o_ref):
    c = c_ref[...]
    act = c * jax.nn.sigmoid(c)
    o_ref[0] = jnp.dot(act, w_ref[0], preferred_element_type=F32,
                       precision=lax.Precision.HIGHEST) + b_ref[0]


def _ada(cc, w_ada, b_ada):
    depth, d, d3 = w_ada.shape
    rows = cc.shape[0]
    col = d
    return pl.pallas_call(
        _ada_kernel,
        out_shape=jax.ShapeDtypeStruct((depth, rows, d3), F32),
        grid=(depth, d3 // col),
        in_specs=[pl.BlockSpec((rows, d), lambda l, j: (0, 0)),
                  pl.BlockSpec((1, d, col), lambda l, j: (l, 0, j)),
                  pl.BlockSpec((1, 1, col), lambda l, j: (l, 0, j))],
        out_specs=pl.BlockSpec((1, rows, col), lambda l, j: (l, 0, j)),
        compiler_params=_params(("arbitrary", "arbitrary")),
        name="ada_mod",
    )(cc, w_ada, b_ada.reshape(depth, 1, d3))


def _s5_disc_kernel(are_ref, aim_ref, ldt_ref, bre_ref, bim_ref, abre_ref, abim_ref, bbre_ref, bbim_ref):
    a_re = are_ref[...]
    a_im = aim_ref[...]
    dt = jnp.exp(ldt_ref[...])
    mag = jnp.exp(a_re * dt)
    ab_re = mag * jnp.cos(a_im * dt)
    ab_im = mag * jnp.sin(a_im * dt)
    den = a_re * a_re + a_im * a_im
    f_re = ((ab_re - 1.0) * a_re + ab_im * a_im) / den
    f_im = (ab_im * a_re - (ab_re - 1.0) * a_im) / den
    b_re = bre_ref[...]
    b_im = bim_ref[...]
    abre_ref[...] = ab_re
    abim_ref[...] = ab_im
    bbre_ref[...] = f_re * b_re - f_im * b_im
    bbim_ref[...] = f_re * b_im + f_im * b_re


def _s5_discretize(a_re, a_im, log_dt, b_re, b_im):
    full = b_re.shape
    rows = b_re.size // LANE
    expand = lambda a: jnp.broadcast_to(a[..., None], full).reshape(rows, LANE)
    ins = [expand(a_re), expand(a_im), expand(jnp.broadcast_to(log_dt[..., None], a_re.shape)),
           b_re.reshape(rows, LANE), b_im.reshape(rows, LANE)]
    outs = pl.pallas_call(
        _s5_disc_kernel,
        out_shape=[jax.ShapeDtypeStruct((rows, LANE), F32)] * 4,
        name="s5_discretize",
    )(*ins)
    ab_re, ab_im, bb_re, bb_im = [o.reshape(full) for o in outs]
    return ab_re[..., 0], ab_im[..., 0], bb_re, bb_im


def _proj_kernel(*refs, kv_lora, q_lora, bw, n_tok, lat_tiles):
    tok_refs = refs[:n_tok]
    (mod_ref, ng_ref, w1_ref, kvg_ref, wk_ref, wv_ref, qg_ref, wq_ref, wqp_ref,
     qgain_ref, qgainp_ref, kgain_ref, kgainp_ref, cos_ref, sin_ref,
     q_out, k_out, vt_out, u_out, xp_out) = refs[n_tok:]
    nb_all, tt, d = tok_refs[0].shape
    x_all = _read_tokens(tok_refs, lat_tiles)
    mod_all = mod_ref[...]
    nb = nb_all // PROJ_ROW_SPLITS
    rows = nb * tt
    inv_qk = 1.0 / QK
    tile_rows = lambda a: jnp.concatenate([a] * nb, axis=0)
    q_cos = tile_rows(cos_ref[...] * (qgain_ref[...] * Q_SCALE))
    q_sin = tile_rows(sin_ref[...] * (qgainp_ref[...] * Q_SCALE))
    k_cos = tile_rows(cos_ref[...] * kgain_ref[...])
    k_sin = tile_rows(sin_ref[...] * kgainp_ref[...])

    staged = []
    for s in range(PROJ_ROW_SPLITS):
        bs = slice(s * nb, (s + 1) * nb)
        mod = mod_all[bs]
        shift, scale = mod[:, 0:1, :], mod[:, 1:2, :]
        h = _rms(x_all[bs], ng_ref[...]) * (1.0 + scale) + shift
        z = _dot(h.reshape(rows, d).astype(BF16), w1_ref[...])
        o = 0
        ckv = z[:, o:o + kv_lora]; o += kv_lora
        kr = z[:, o:o + HEAD_PAD]; o += HEAD_PAD
        kp = z[:, o:o + HEAD_PAD]; o += HEAD_PAD
        cq = z[:, o:o + q_lora]; o += q_lora
        u_out[:, bs, :] = jnp.transpose(z[:, o:o + bw].reshape(nb, tt, bw), (1, 0, 2)); o += bw
        xp_out[bs] = z[:, o:o + 2 * bw].reshape(nb, tt, 2 * bw)
        cqn = _rms(cq, qg_ref[...]).astype(BF16)
        ckvn = _rms(ckv, kvg_ref[...]).astype(BF16)
        staged.append((bs, kr, kp, _dot(cqn, wq_ref[...]), _dot(cqn, wqp_ref[...]),
                       _dot(ckvn, wk_ref[...]), _dot(ckvn, wv_ref[...])))

    for bs, kr, kp, qraw, qprt, knope, v in staged:
        for hh in range(HEADS):
            sl = slice(hh * HEAD_PAD, (hh + 1) * HEAD_PAD)
            qh = qraw[:, sl]
            rs = lax.rsqrt(jnp.sum(qh * qh, axis=-1, keepdims=True) * inv_qk + EPS)
            q_out[bs, hh] = ((qh * q_cos + qprt[:, sl] * q_sin) * rs).astype(BF16).reshape(nb, tt, HEAD_PAD)
        kp_sin = kp * k_sin
        for hh in range(HEADS):
            sl = slice(hh * HEAD_PAD, (hh + 1) * HEAD_PAD)
            kh = knope[:, sl] + kr
            rs = lax.rsqrt(jnp.sum(kh * kh, axis=-1, keepdims=True) * inv_qk + EPS)
            k_out[bs, hh] = ((kh * k_cos + kp_sin) * rs).astype(BF16).reshape(nb, tt, HEAD_PAD)
        ones_rows = (lax.broadcasted_iota(jnp.int32, (HEADS, VAUG - VDIM, tt), 1) == 0).astype(BF16)
        for bi in range(nb):
            vt = v[bi * tt:(bi + 1) * tt].T.astype(BF16).reshape(HEADS, VDIM, tt)
            vt_out[bs.start + bi] = jnp.concatenate([vt, ones_rows], axis=1)


def _proj(toks, mod3, w, li, cos_t, sin_t, n_lat):
    b, _, d = toks[0].shape
    t = sum(a.shape[1] for a in toks)
    nb, tt = SUBLANE, PROJ_STEPS
    bw = d // N_BRANCH
    kv_lora, q_lora = w["kv_norm"].shape[-1], w["q_norm"].shape[-1]
    ctx_group = mod3.shape[1] - 1
    lat_tiles = n_lat // tt
    mod_map = lambda i, j: (li, jnp.where(j < lat_tiles, i, ctx_group), 0, 0, 0)
    kern = functools.partial(_proj_kernel, kv_lora=kv_lora, q_lora=q_lora, bw=bw, n_tok=len(toks),
                             lat_tiles=lat_tiles)
    names = ["norm_g", "w1", "kv_norm", "wk", "wv", "q_norm", "wq", "wqp", "qgain", "qgainp", "kgain", "kgainp"]
    return pl.pallas_call(
        kern,
        out_shape=[jax.ShapeDtypeStruct((b, HEADS, t, HEAD_PAD), BF16),
                   jax.ShapeDtypeStruct((b, HEADS, t, HEAD_PAD), BF16),
                   jax.ShapeDtypeStruct((b, HEADS, VAUG, t), BF16),
                   jax.ShapeDtypeStruct((t, b, bw), F32),
                   jax.ShapeDtypeStruct((b, t, 2 * bw), F32)],
        grid=(b // nb, t // tt),
        in_specs=_token_specs(toks, nb, tt, lat_tiles)
                 + [pl.BlockSpec((None, None, nb, 3, d), mod_map)]
                 + [_wspec(w[n], (li,)) for n in names]
                 + [pl.BlockSpec((tt, HEAD_PAD), lambda i, j: (j, 0)),
                    pl.BlockSpec((tt, HEAD_PAD), lambda i, j: (j, 0))],
        out_specs=[pl.BlockSpec((nb, HEADS, tt, HEAD_PAD), lambda i, j: (i, 0, j, 0)),
                   pl.BlockSpec((nb, HEADS, tt, HEAD_PAD), lambda i, j: (i, 0, j, 0)),
                   pl.BlockSpec((nb, HEADS, VAUG, tt), lambda i, j: (i, 0, 0, j)),
                   pl.BlockSpec((tt, nb, bw), lambda i, j: (j, i, 0)),
                   pl.BlockSpec((nb, tt, 2 * bw), lambda i, j: (i, j, 0))],
        compiler_params=_params(("arbitrary", "arbitrary")),
        name="proj_qkv",
    )(*toks, mod3, *[w[n] for n in names], cos_t, sin_t)


def _seq_kernel(xp_ref, cw_ref, cb_ref, pw_ref, pb_ref, ps_ref, xl_out, op_out,
                pad_ref, f2_ref, f4_ref, f8_ref, inv_ref, *, segs, bw):
    assert POOL_WINDOWS == (2, 4, 8, 16) and bw == 2 * LANE and HALO == 8
    gw = bw // len(POOL_WINDOWS)
    cw = cw_ref[...]
    left = LRU_CONV // 2

    @pl.when(pl.program_id(0) == 0)
    def _():
        for lo, t in segs:
            lane = lax.broadcasted_iota(jnp.int32, (t, bw), 1)
            row = lax.broadcasted_iota(jnp.int32, (t, bw), 0)
            half = jnp.ones((t, bw), jnp.int32)
            for gi in range(1, len(POOL_WINDOWS)):
                half = jnp.where(lane >= gi * gw, POOL_WINDOWS[gi] // 2, half)
            cnt = (jnp.minimum(row + half, t) - jnp.maximum(row - half, 0)).astype(F32)
            inv_ref[lo:lo + t, :] = 1.0 / cnt

    lane_half = lax.broadcasted_iota(jnp.int32, (1, LANE), 1) < gw
    for lo, t in segs:
        n = t + 2 * HALO
        pad_ref[0:HALO, :] = jnp.zeros((HALO, 2 * bw), F32)
        pad_ref[HALO + t:n + HALO, :] = jnp.zeros((2 * HALO, 2 * bw), F32)
        pad_ref[HALO:HALO + t, :] = xp_ref[0, lo:lo + t, :]
        f2_ref[n:n + HALO, :] = jnp.zeros((HALO, bw), F32)
        f4_ref[n:n + HALO, :] = jnp.zeros((HALO, bw), F32)
        f8_ref[n:n + HALO, :] = jnp.zeros((HALO, LANE), F32)

        acc = cb_ref[...] + pad_ref[HALO - left:HALO - left + t, 0:bw] * cw[0:1]
        for k in range(1, LRU_CONV):
            acc = acc + pad_ref[HALO + k - left:HALO + k - left + t, 0:bw] * cw[k:k + 1]
        xl_out[0, lo:lo + t, :] = acc

        f2_ref[0:n, :] = pad_ref[0:n, bw:2 * bw] + pad_ref[1:n + 1, bw:2 * bw]
        f4_ref[0:n, :] = f2_ref[0:n, :] + f2_ref[2:n + 2, :]
        f8_ref[0:n, :] = f4_ref[0:n, LANE:bw] + f4_ref[4:n + 4, LANE:bw]
        sel_lo = jnp.where(lane_half, f2_ref[7:7 + t, 0:LANE], f4_ref[6:6 + t, 0:LANE])
        sel_hi = jnp.where(lane_half, f8_ref[4:4 + t, :], f8_ref[0:t, :] + f8_ref[HALO:HALO + t, :])
        sel = jnp.concatenate([sel_lo, sel_hi], axis=1)
        p = sel * inv_ref[lo:lo + t, :] - pad_ref[HALO:HALO + t, bw:2 * bw]
        y = _dot(p.astype(BF16), pw_ref[...]) + pb_ref[...]
        op_out[0, lo:lo + t, :] = y * ps_ref[...]


def _seq(xp, w, li, segs):
    b, t, w2 = xp.shape
    bw = w2 // 2
    kern = functools.partial(_seq_kernel, segs=segs, bw=bw)
    names = ["conv_w", "conv_b", "pool_w", "pool_b", "pool_scale"]
    max_t = max(n for _, n in segs)
    return pl.pallas_call(
        kern,
        out_shape=[jax.ShapeDtypeStruct((b, t, bw), F32), jax.ShapeDtypeStruct((b, t, bw), F32)],
        grid=(b,),
        in_specs=[pl.BlockSpec((1, t, w2), lambda i: (i, 0, 0))] + [_wspec(w[n], (li,)) for n in names],
        out_specs=[pl.BlockSpec((1, t, bw), lambda i: (i, 0, 0)),
                   pl.BlockSpec((1, t, bw), lambda i: (i, 0, 0))],
        scratch_shapes=[pltpu.VMEM((max_t + 3 * HALO, w2), F32), pltpu.VMEM((max_t + 3 * HALO, bw), F32),
                        pltpu.VMEM((max_t + 3 * HALO, bw), F32), pltpu.VMEM((max_t + 3 * HALO, LANE), F32),
                        pltpu.VMEM((t, bw), F32)],
        compiler_params=_params(("arbitrary",)),
        name="conv_pool",
    )(xp, *[w[n] for n in names])


def _attn_kernel(q_ref, k_ref, vt_ref, o_ref):
    n_k = k_ref.shape[2]
    ck = min(KEY_CHUNK, n_k)
    nch = n_k // ck
    items = [(hh, c) for hh in range(HEADS) for c in range(nch)]

    def scores(item):
        hh, c = item
        return _dot_nt(k_ref[0, hh, c * ck:(c + 1) * ck, :], q_ref[0, hh])

    outs = []
    m = acc = None
    ahead = [scores(it) for it in items[:ATTN_LOOKAHEAD]]
    for i, (hh, c) in enumerate(items):
        sc = ahead.pop(0)
        if i + ATTN_LOOKAHEAD < len(items):
            ahead.append(scores(items[i + ATTN_LOOKAHEAD]))
        ks = slice(c * ck, (c + 1) * ck)
        mc = jnp.max(sc, axis=0, keepdims=True)
        if c == 0:
            m_new = mc
            acc = _dot(vt_ref[0, hh, :, ks], jnp.exp2(sc - m_new).astype(BF16))
        else:
            m_new = jnp.maximum(m, mc)
            acc = jnp.exp2(m - m_new) * acc + _dot(vt_ref[0, hh, :, ks], jnp.exp2(sc - m_new).astype(BF16))
        m = m_new
        if c == nch - 1:
            outs.append(acc[0:VDIM] / acc[VDIM:VDIM + 1])
    o_ref[0] = jnp.concatenate(outs, axis=0).T


def _attend(q, k, vt, n_q, q_off, n_k, k_off, tq):
    b = q.shape[0]
    hv = HEADS * VDIM
    assert n_q % tq == 0 and q_off % tq == 0 and k_off % n_k == 0
    qo, ko = q_off // tq, k_off // n_k
    return pl.pallas_call(
        _attn_kernel,
        out_shape=jax.ShapeDtypeStruct((b, n_q, hv), F32),
        grid=(b, n_q // tq),
        in_specs=[pl.BlockSpec((1, HEADS, tq, HEAD_PAD), lambda i, j: (i, 0, j + qo, 0)),
                  pl.BlockSpec((1, HEADS, n_k, HEAD_PAD), lambda i, j: (i, 0, ko, 0)),
                  pl.BlockSpec((1, HEADS, VAUG, n_k), lambda i, j: (i, 0, 0, ko))],
        out_specs=pl.BlockSpec((1, tq, hv), lambda i, j: (i, j, 0)),
        compiler_params=_params(("arbitrary", "arbitrary")),
        name="attention",
    )(q, k, vt)


def _scan_kernel(uf_ref, ub_ref, xf_ref, xb_ref, bmat_ref, cre_ref, cim_ref, ar_ref, ai_ref,
                 wa_ref, ba_ref, wx_ref, bx_ref, lam_ref, yf_ref, yb_ref,
                 hbuf, abuf, bbuf, hr_s, hi_s, hl_s, *, tc, bsz, bw, ns):
    r = tc * bsz
    sb = SCAN_SUB
    nsub = tc // sb
    rs = sb * bsz
    nblk = bw // LANE
    sblk = ns // nblk
    u_refs, x_refs, y_refs = (uf_ref, ub_ref), (xf_ref, xb_ref), (yf_ref, yb_ref)

    @pl.when(pl.program_id(0) == 0)
    def _():
        hr_s[...] = jnp.zeros_like(hr_s)
        hi_s[...] = jnp.zeros_like(hi_s)
        hl_s[...] = jnp.zeros_like(hl_s)

    u16 = [u_refs[d][...].reshape(r, bw).astype(BF16) for d in range(2)]
    xls = [jnp.transpose(x_refs[d][...], (1, 0, 2)).reshape(r, bw) for d in range(2)]
    softplus = []
    for d in range(2):
        nl = -lam_ref[d]
        softplus.append(jnp.maximum(nl, 0.0) + jnp.log1p(jnp.exp(-jnp.abs(nl))))

    def sub_rows(d, k):
        kk = (nsub - 1 - k) if d else k
        return kk, slice(kk * rs, (kk + 1) * rs)

    def input_stage(k):
        for d in range(2):
            _, rw = sub_rows(d, k)
            hbuf[d, rw, :] = _dot(u16[d][rw], bmat_ref[d])
            xl = xls[d][rw]
            xb = xl.astype(BF16)
            rg = jax.nn.sigmoid(_dot(xb, wa_ref[d]) + ba_ref[d])
            ig = jax.nn.sigmoid(_dot(xb, wx_ref[d]) + bx_ref[d])
            log_a = (-LRU_C) * rg * softplus[d]
            abuf[d, rw, :] = jnp.exp(log_a)
            th = jnp.tanh(log_a)
            gain2 = -2.0 * th / (1.0 - th)
            gain = jnp.where(gain2 > 0.0, gain2 * lax.rsqrt(gain2), 0.0)
            bbuf[d, rw, :] = gain * (ig * xl)

    w = ns // SCAN_LANE_SPLITS
    lanes = [(slice(j * w, (j + 1) * w), slice(ns + j * w, ns + (j + 1) * w)) for j in range(SCAN_LANE_SPLITS)]
    coef = [[(jnp.broadcast_to(ar_ref[d, :, re_sl], (bsz, w)), jnp.broadcast_to(ai_ref[d, :, re_sl], (bsz, w)))
             for re_sl, _ in lanes] for d in range(2)]
    hl = [hl_s[d] for d in range(2)]

    def scan_stage(k):
        def step_rows(t, d):
            te = (tc - 1 - t) if d else t
            return te, slice(te * bsz, (te + 1) * bsz)

        for t in range(k * sb, (k + 1) * sb):
            for d in range(2):
                te, rw = step_rows(t, d)
                hl[d] = abuf[d, rw, :] * hl[d] + bbuf[d, rw, :]
                y_refs[d][te, :, bw:2 * bw] = hl[d]
        for j, (re_sl, im_sl) in enumerate(lanes):
            for t in range(k * sb, (k + 1) * sb):
                for d in range(2):
                    _, rw = step_rows(t, d)
                    if t == 0:
                        phr, phi = hr_s[d, :, re_sl], hi_s[d, :, re_sl]
                    else:
                        _, prw = step_rows(t - 1, d)
                        phr, phi = hbuf[d, prw, re_sl], hbuf[d, prw, im_sl]
                    ar, ai = coef[d][j]
                    hbuf[d, rw, re_sl] = ar * phr - ai * phi + hbuf[d, rw, re_sl]
                    hbuf[d, rw, im_sl] = ar * phi + ai * phr + hbuf[d, rw, im_sl]

    def readout_stage(k):
        for d in range(2):
            kk, rw = sub_rows(d, k)
            for jb in range(nblk):
                st = slice(jb * sblk, (jb + 1) * sblk)
                sti = slice(ns + jb * sblk, ns + (jb + 1) * sblk)
                ln = slice(jb * LANE, (jb + 1) * LANE)
                y = (_dot(hbuf[d, rw, st].astype(BF16), cre_ref[d, st, ln])
                     - _dot(hbuf[d, rw, sti].astype(BF16), cim_ref[d, st, ln]))
                y_refs[d][kk * sb:(kk + 1) * sb, :, ln] = y.reshape(sb, bsz, LANE)

    lead = min(2, nsub)
    for k in range(lead):
        input_stage(k)
    for k in range(nsub):
        if k + lead < nsub:
            input_stage(k + lead)
        scan_stage(k)
        if k >= 1:
            readout_stage(k - 1)
    readout_stage(nsub - 1)

    for d in range(2):
        hl_s[d] = hl[d]
        last = slice((0 if d else tc - 1) * bsz, (1 if d else tc) * bsz)
        hr_s[d] = hbuf[d, last, 0:ns]
        hi_s[d] = hbuf[d, last, ns:2 * ns]


def _scan(u_tm, xl, w, li, n_lat):
    t, bsz, bw = u_tm.shape
    tc = SCAN_CHUNK
    nc = t // tc
    ncl = n_lat // tc
    ncc = nc - ncl
    ns = w["ar"].shape[-1]
    fwd = lambda s: jnp.where(s < ncc, ncl + s, s - ncc)
    bwd = lambda s: nc - 1 - s
    names = ["bmat", "cre", "cim", "ar", "ai", "wa", "ba", "wx", "bx", "lam"]
    r = tc * bsz
    kern = functools.partial(_scan_kernel, tc=tc, bsz=bsz, bw=bw, ns=ns)
    return pl.pallas_call(
        kern,
        out_shape=[jax.ShapeDtypeStruct((t, bsz, 2 * bw), F32)] * 2,
        grid=(nc,),
        in_specs=[pl.BlockSpec((tc, bsz, bw), lambda s: (fwd(s), 0, 0)),
                  pl.BlockSpec((tc, bsz, bw), lambda s: (bwd(s), 0, 0)),
                  pl.BlockSpec((bsz, tc, bw), lambda s: (0, fwd(s), 0)),
                  pl.BlockSpec((bsz, tc, bw), lambda s: (0, bwd(s), 0))]
                 + [_wspec(w[n], (li,)) for n in names],
        out_specs=[pl.BlockSpec((tc, bsz, 2 * bw), lambda s: (fwd(s), 0, 0)),
                   pl.BlockSpec((tc, bsz, 2 * bw), lambda s: (bwd(s), 0, 0))],
        scratch_shapes=[pltpu.VMEM((2, r, 2 * ns), F32), pltpu.VMEM((2, r, bw), F32), pltpu.VMEM((2, r, bw), F32),
                        pltpu.VMEM((2, bsz, ns), F32), pltpu.VMEM((2, bsz, ns), F32),
                        pltpu.VMEM((2, bsz, bw), F32)],
        compiler_params=_params(("arbitrary",)),
        name="scan_bidir",
    )(u_tm, u_tm, xl, xl, *[w[n] for n in names])


def _merge_kernel(*refs, bw, lat_tiles, with_ctx, n_tok):
    tok_refs, refs = refs[:n_tok], refs[n_tok:]
    if with_ctx:
        (mod_ref, ng_ref, om_ref, omc_ref, yf_ref, yb_ref, u_ref, op_ref,
         w2_ref, wb_ref, wo_ref, wglu_ref, d_ref, out_ref) = refs
    else:
        (mod_ref, ng_ref, om_ref, yf_ref, yb_ref, u_ref, op_ref,
         w2_ref, wb_ref, wo_ref, wglu_ref, d_ref, out_ref) = refs
    nb_all, tt, d = tok_refs[0].shape
    x_all = _read_tokens(tok_refs, lat_tiles)
    mod_all = mod_ref[...]
    rr_all = jnp.transpose(yf_ref[...] + yb_ref[...], (1, 0, 2))
    u_all = jnp.transpose(u_ref[...], (1, 0, 2))
    o_mla_all = om_ref[...]
    if with_ctx:
        o_mla_all = jnp.where(pl.program_id(1) >= lat_tiles, omc_ref[...], o_mla_all)
    op_all = op_ref[...]

    nb = nb_all // MERGE_ROW_SPLITS
    rows = nb * tt
    xs, hs, gates, branches = [], [], [], []
    for s in range(MERGE_ROW_SPLITS):
        bs = slice(s * nb, (s + 1) * nb)
        x = x_all[bs]
        mod = mod_all[bs]
        shift, scale, gate = mod[:, 0:1, :], mod[:, 1:2, :], mod[:, 2:3, :]
        h = (_rms(x, ng_ref[...]) * (1.0 + scale) + shift).reshape(rows, d).astype(BF16)
        rr = rr_all[bs].reshape(rows, 2 * bw)
        u = u_all[bs].reshape(rows, bw)
        g = jax.nn.gelu(rr[:, 0:bw] + d_ref[...] * u, approximate=True)
        o_s5 = g * jax.nn.sigmoid(_dot(g.astype(BF16), wglu_ref[...]))
        xs.append(x)
        hs.append(h)
        gates.append(gate)
        branches.append((o_mla_all[bs].reshape(rows, bw), o_s5, rr[:, bw:2 * bw], op_all[bs].reshape(rows, bw)))

    ys = [None] * MERGE_ROW_SPLITS
    for n in range(N_BRANCH):
        for s in range(MERGE_ROW_SPLITS):
            gp = _dot(hs[s], w2_ref[:, n * bw:(n + 1) * bw])
            off = N_BRANCH * bw + n * d
            ml = _dot(hs[s], w2_ref[:, off:off + d])
            tn = _dot((branches[s][n] * (gp * jax.nn.sigmoid(gp))).astype(BF16), wb_ref[n])
            term = jax.nn.sigmoid(ml) * tn
            ys[s] = term if ys[s] is None else ys[s] + term
    for s in range(MERGE_ROW_SPLITS):
        bs = slice(s * nb, (s + 1) * nb)
        out_ref[bs] = xs[s] + gates[s] * _dot(ys[s].astype(BF16), wo_ref[...]).reshape(nb, tt, d)


def _merge(toks, mod3, o_lat, o_ctx, yf, yb, u_tm, o_pool, w, li, n_lat, n_out):
    b, _, d = toks[0].shape
    nb, tt = SUBLANE, MERGE_STEPS
    bw = d // N_BRANCH
    with_ctx = o_ctx is not None
    lat_tiles = n_lat // tt
    ctx_group = mod3.shape[1] - 1
    mod_map = lambda i, j: (li, jnp.where(j < lat_tiles, i, ctx_group), 0, 0, 0)
    bm = lambda width: pl.BlockSpec((nb, tt, width), lambda i, j: (i, j, 0))
    tm = lambda width: pl.BlockSpec((tt, nb, width), lambda i, j: (j, i, 0))
    names = ["w2", "wb", "wo", "wglu", "s5_d"]
    in_specs = _token_specs(toks, nb, tt, lat_tiles) + [
        pl.BlockSpec((None, None, nb, 3, d), mod_map), _wspec(w["norm_g"], (li,)),
        pl.BlockSpec((nb, tt, bw), lambda i, j: (i, jnp.minimum(j, lat_tiles - 1), 0))]
    args = [*toks, mod3, w["norm_g"], o_lat]
    if with_ctx:
        ctx_tiles = o_ctx.shape[1] // tt
        in_specs.append(pl.BlockSpec((nb, tt, bw), lambda i, j: (i, jnp.clip(j - lat_tiles, 0, ctx_tiles - 1), 0)))
        args.append(o_ctx)
    in_specs += [tm(2 * bw), tm(2 * bw), tm(bw), bm(bw)] + [_wspec(w[n], (li,)) for n in names]
    args += [yf, yb, u_tm, o_pool] + [w[n] for n in names]
    kern = functools.partial(_merge_kernel, bw=bw, lat_tiles=lat_tiles, with_ctx=with_ctx, n_tok=len(toks))
    return pl.pallas_call(
        kern,
        out_shape=jax.ShapeDtypeStruct((b, n_out, d), F32),
        grid=(b // nb, n_out // tt),
        in_specs=in_specs,
        out_specs=bm(d),
        compiler_params=_params(("arbitrary", "arbitrary")),
        name="merge_out",
    )(*args)


def _block_diag(blocks):
    n, a, b = blocks.shape[-3:]
    eye = jnp.eye(n, dtype=blocks.dtype)
    return jnp.einsum("...nab,nm->...namb", blocks, eye).reshape(blocks.shape[:-3] + (n * a, n * b))


def _rope_perm():
    half = ROPE // 2
    nf = half // 2
    base = jnp.concatenate([jnp.arange(nf, half), jnp.arange(0, nf)])
    return jnp.concatenate([base, base + half])


def _head_pad(a, lo):
    n = a.shape[-1]
    pad = [(0, 0)] * (a.ndim - 1) + [(lo, HEAD_PAD - lo - n)]
    return jnp.pad(a, pad)


def _rope_tables(n_lat, n_ctx):
    rows_n = n_lat // GRID_W
    row = jnp.repeat(jnp.arange(rows_n, dtype=jnp.int32), GRID_W).astype(F32)
    col = jnp.tile(jnp.arange(GRID_W, dtype=jnp.int32), rows_n).astype(F32)
    nf = ROPE // 4
    inv = ROPE_THETA ** (-jnp.arange(nf, dtype=F32) / nf)
    ang_r = row[:, None] * inv
    ang_c = col[:, None] * inv
    cr, sr, cc, sc = jnp.cos(ang_r), jnp.sin(ang_r), jnp.cos(ang_c), jnp.sin(ang_c)
    ones = jnp.ones((n_lat, NOPE), F32)
    zeros = jnp.zeros((n_lat, NOPE), F32)
    tail = HEAD_PAD - QK
    cos = jnp.concatenate([ones, cr, cr, cc, cc, jnp.ones((n_lat, tail), F32)], -1)
    sin = jnp.concatenate([zeros, -sr, sr, -sc, sc, jnp.zeros((n_lat, tail), F32)], -1)
    cos = jnp.concatenate([cos, jnp.ones((n_ctx, HEAD_PAD), F32)], 0)
    sin = jnp.concatenate([sin, jnp.zeros((n_ctx, HEAD_PAD), F32)], 0)
    return cos, sin


def _prep_weights(p, disc):
    w_in = p["w_in"]
    depth, d, _ = w_in.shape
    bw = d // N_BRANCH
    kv_lora = p["mla_kv_norm"].shape[-1]
    q_lora = p["mla_q_norm"].shape[-1]
    perm = _rope_perm()
    off_krope = kv_lora
    off_s5 = off_krope + ROPE
    off_lru = off_s5 + bw
    off_cq = off_lru + bw
    off_pool = off_cq + q_lora
    off_gate = off_pool + bw
    w_mix = w_in[:, :, :off_gate].astype(BF16)
    w_krope = w_mix[:, :, off_krope:off_s5]
    w1 = jnp.concatenate([
        w_mix[:, :, :kv_lora],
        _head_pad(w_krope, NOPE),
        _head_pad(w_krope[:, :, perm], NOPE),
        w_mix[:, :, off_cq:off_pool],
        w_mix[:, :, off_s5:off_lru],
        w_mix[:, :, off_lru:off_cq],
        w_mix[:, :, off_pool:off_gate]], axis=2)
    w2 = w_in[:, :, off_gate:].astype(BF16)

    w_uq = p["mla_w_uq"].reshape(depth, q_lora, HEADS, QK)
    wq = _head_pad(w_uq, 0).reshape(depth, q_lora, HEADS * HEAD_PAD).astype(BF16)
    wqp = _head_pad(w_uq[..., NOPE:][..., perm], NOPE).reshape(depth, q_lora, HEADS * HEAD_PAD).astype(BF16)
    w_ukv = p["mla_w_ukv"].reshape(depth, kv_lora, HEADS, NOPE + VDIM)
    wk = _head_pad(w_ukv[..., :NOPE], 0).reshape(depth, kv_lora, HEADS * HEAD_PAD).astype(BF16)
    wv = w_ukv[..., NOPE:].reshape(depth, kv_lora, HEADS * VDIM).astype(BF16)
    q_gain, k_gain = p["mla_q_gain"], p["mla_k_gain"]
    row = lambda a: a[..., None, :]

    ab_re, ab_im, bb_re, bb_im = disc
    sw = lambda a: jnp.swapaxes(a, -1, -2)
    bmat = jnp.concatenate([_block_diag(sw(bb_re)), _block_diag(sw(bb_im))], axis=-1).astype(BF16)
    return dict(
        norm_g=row(p["norm_g"]), w1=w1, w2=w2,
        kv_norm=row(p["mla_kv_norm"]), q_norm=row(p["mla_q_norm"]),
        wq=wq, wqp=wqp, wk=wk, wv=wv,
        qgain=row(_head_pad(q_gain, 0)), qgainp=row(_head_pad(q_gain[:, NOPE:][:, perm], NOPE)),
        kgain=row(_head_pad(k_gain, 0)), kgainp=row(_head_pad(k_gain[:, NOPE:][:, perm], NOPE)),
        conv_w=p["lru_conv_w"], conv_b=row(p["lru_conv_b"]),
        pool_w=_block_diag(p["pool_w"]).astype(BF16), pool_b=row(p["pool_b"]),
        pool_scale=row(p["pool_scale"]),
        wb=p["w_branch"].astype(BF16), wo=p["w_out"].astype(BF16),
        wglu=p["s5_w_glu"].astype(BF16), s5_d=row(p["s5_d"]),
        bmat=bmat,
        cre=_block_diag(sw(p["s5_c_re"])).astype(BF16), cim=_block_diag(sw(p["s5_c_im"])).astype(BF16),
        ar=row(ab_re.reshape(ab_re.shape[:2] + (-1,))), ai=row(ab_im.reshape(ab_im.shape[:2] + (-1,))),
        wa=_block_diag(p["lru_w_a"]).astype(BF16), ba=row(p["lru_b_a"]),
        wx=_block_diag(p["lru_w_x"]).astype(BF16), bx=row(p["lru_b_x"]),
        lam=row(p["lru_lambda"]))


def kernel(x, c, ctx, c_ctx, w_ada, b_ada, norm_g, w_in, mla_q_norm, mla_kv_norm, mla_w_uq, mla_w_ukv, mla_q_gain, mla_k_gain, s5_a_re, s5_a_im, s5_log_dt, s5_b_re, s5_b_im, s5_c_re, s5_c_im, s5_d, s5_w_glu, lru_conv_w, lru_conv_b, lru_lambda, lru_w_a, lru_b_a, lru_w_x, lru_b_x, pool_w, pool_b, pool_scale, w_branch, w_out):
    p = dict(w_in=w_in, norm_g=norm_g, mla_q_norm=mla_q_norm, mla_kv_norm=mla_kv_norm,
             mla_w_uq=mla_w_uq, mla_w_ukv=mla_w_ukv, mla_q_gain=mla_q_gain, mla_k_gain=mla_k_gain,
             s5_c_re=s5_c_re, s5_c_im=s5_c_im, s5_d=s5_d, s5_w_glu=s5_w_glu,
             lru_conv_w=lru_conv_w, lru_conv_b=lru_conv_b, lru_lambda=lru_lambda,
             lru_w_a=lru_w_a, lru_b_a=lru_b_a, lru_w_x=lru_w_x, lru_b_x=lru_b_x,
             pool_w=pool_w, pool_b=pool_b, pool_scale=pool_scale, w_branch=w_branch, w_out=w_out)
    depth = w_in.shape[0]
    b, l, d = x.shape
    lc = ctx.shape[1]
    t = l + lc
    assert b % SUBLANE == 0 and l % PROJ_STEPS == 0 and lc % PROJ_STEPS == 0 and l % lc == 0

    n_rows = -(-(b + 1) // SUBLANE) * SUBLANE
    cc = jnp.concatenate([c, c_ctx[None], jnp.zeros((n_rows - b - 1, d), F32)], axis=0)
    mod_all = _ada(cc, w_ada, b_ada)
    mod3 = jnp.concatenate([
        mod_all[:, :b].reshape(depth, b // SUBLANE, SUBLANE, 3, d),
        jnp.broadcast_to(mod_all[:, b:b + 1].reshape(depth, 1, 1, 3, d), (depth, 1, SUBLANE, 3, d))], axis=1)
    disc = _s5_discretize(s5_a_re, s5_a_im, s5_log_dt, s5_b_re, s5_b_im)
    w = _prep_weights(p, disc)
    cos_t, sin_t = _rope_tables(l, lc)
    tq_l, tq_c = min(Q_TILE, l), min(Q_TILE, lc)

    toks = (x, ctx)
    for li in range(depth):
        with_ctx = li < depth - 1
        q, k, vt, u_tm, xp = _proj(toks, mod3, w, li, cos_t, sin_t, l)
        xl, o_pool = _seq(xp, w, li, ((0, l), (l, lc)))
        o_lat = _attend(q, k, vt, l, 0, t, 0, tq_l)
        o_ctx = _attend(q, k, vt, lc, l, lc, l, tq_c) if with_ctx else None
        yf, yb = _scan(u_tm, xl, w, li, l)
        toks = (_merge(toks, mod3, o_lat, o_ctx, yf, yb, u_tm, o_pool, w, li, l, t if with_ctx else l),)
    return toks[0]
```

```python
import functools
import math

import jax
import jax.numpy as jnp
from jax import lax
from jax.experimental import pallas as pl
from jax.experimental.pallas import tpu as pltpu

F32 = jnp.float32
BF16 = jnp.bfloat16

EPS = 1e-6
GRID_W = 64
N_BRANCH = 4
HEADS = 4
NOPE = 64
ROPE = 32
VDIM = 64
VAUG = 80
QK = NOPE + ROPE
ROPE_THETA = 10000.0
LRU_CONV = 4
LRU_C = 8.0
POOL_WINDOWS = (2, 4, 8, 16)
Q_SCALE = (QK ** -0.5) * math.log2(math.e)

LANE = 128
SUBLANE = 8
HEAD_PAD = LANE
VMEM_LIMIT = 56 * 1024 * 1024

PROJ_STEPS = 128
PROJ_ROW_SPLITS = 2
MERGE_STEPS = 64
MERGE_ROW_SPLITS = 2
Q_TILE = 512
KEY_CHUNK = 256
ATTN_LOOKAHEAD = 2
SCAN_CHUNK = 64
SCAN_SUB = 16
SCAN_LANE_SPLITS = 2
HALO = 8


def _dot(a, b):
    return jnp.dot(a, b, preferred_element_type=F32)


def _dot_nt(a, b):
    return lax.dot_general(a, b, (((1,), (1,)), ((), ())), preferred_element_type=F32)


def _rms(x, g):
    return x * lax.rsqrt(jnp.mean(x * x, axis=-1, keepdims=True) + EPS) * g


def _wspec(arr, lead):
    tail = arr.shape[len(lead):]
    idx = tuple(lead) + (0,) * len(tail)
    return pl.BlockSpec((None,) * len(lead) + tail, lambda *_: idx, pipeline_mode=pl.Buffered(1))


def _token_specs(toks, nb, tt, lat_tiles):
    if len(toks) == 1:
        return [pl.BlockSpec((nb, tt, toks[0].shape[2]), lambda i, j: (i, j, 0))]
    lat, ctx = toks
    ctx_tiles = ctx.shape[1] // tt
    return [pl.BlockSpec((nb, tt, lat.shape[2]), lambda i, j: (i, jnp.minimum(j, lat_tiles - 1), 0)),
            pl.BlockSpec((nb, tt, ctx.shape[2]), lambda i, j: (i, jnp.clip(j - lat_tiles, 0, ctx_tiles - 1), 0))]


def _read_tokens(tok_refs, lat_tiles):
    if len(tok_refs) == 1:
        return tok_refs[0][...]
    return jnp.where(pl.program_id(1) >= lat_tiles, tok_refs[1][...], tok_refs[0][...])


def _params(sem):
    return pltpu.CompilerParams(dimension_semantics=sem, vmem_limit_bytes=VMEM_LIMIT)


def _ada_kernel(c_ref, w_ref, b_ref, o_ref):
    c = c_ref[...]
    act = c * jax.nn.sigmoid(c)
    o_ref[0] = jnp.dot(act, w_ref[0], preferred_element_type=F32,
                       precision=lax.Precision.HIGHEST) + b_ref[0]


def _ada(cc, w_ada, b_ada):
    depth, d, d3 = w_ada.shape
    rows = cc.shape[0]
    col = d
    return pl.pallas_call(
        _ada_kernel,
        out_shape=jax.ShapeDtypeStruct((depth, rows, d3), F32),
        grid=(depth, d3 // col),
        in_specs=[pl.BlockSpec((rows, d), lambda l, j: (0, 0)),
                  pl.BlockSpec((1, d, col), lambda l, j: (l, 0, j)),
                  pl.BlockSpec((1, 1, col), lambda l, j: (l, 0, j))],
        out_specs=pl.BlockSpec((1, rows, col), lambda l, j: (l, 0, j)),
        compiler_params=_params(("arbitrary", "arbitrary")),
        name="ada_mod",
    )(cc, w_ada, b_ada.reshape(depth, 1, d3))


def _s5_disc_kernel(are_ref, aim_ref, ldt_ref, bre_ref, bim_ref, abre_ref, abim_ref, bbre_ref, bbim_ref):
    a_re = are_ref[...]
    a_im = aim_ref[...]
    dt = jnp.exp(ldt_ref[...])
    mag = jnp.exp(a_re * dt)
    ab_re = mag * jnp.cos(a_im * dt)
    ab_im = mag * jnp.sin(a_im * dt)
    den = a_re * a_re + a_im * a_im
    f_re = ((ab_re - 1.0) * a_re + ab_im * a_im) / den
    f_im = (ab_im * a_re - (ab_re - 1.0) * a_im) / den
    b_re = bre_ref[...]
    b_im = bim_ref[...]
    abre_ref[...] = ab_re
    abim_ref[...] = ab_im
    bbre_ref[...] = f_re * b_re - f_im * b_im
    bbim_ref[...] = f_re * b_im + f_im * b_re


def _s5_discretize(a_re, a_im, log_dt, b_re, b_im):
    full = b_re.shape
    rows = b_re.size // LANE
    expand = lambda a: jnp.broadcast_to(a[..., None], full).reshape(rows, LANE)
    ins = [expand(a_re), expand(a_im), expand(jnp.broadcast_to(log_dt[..., None], a_re.shape)),
           b_re.reshape(rows, LANE), b_im.reshape(rows, LANE)]
    outs = pl.pallas_call(
        _s5_disc_kernel,
        out_shape=[jax.ShapeDtypeStruct((rows, LANE), F32)] * 4,
        name="s5_discretize",
    )(*ins)
    ab_re, ab_im, bb_re, bb_im = [o.reshape(full) for o in outs]
    return ab_re[..., 0], ab_im[..., 0], bb_re, bb_im


def _proj_kernel(*refs, kv_lora, q_lora, bw, n_tok, lat_tiles):
    tok_refs = refs[:n_tok]
    (mod_ref, ng_ref, w1_ref, kvg_ref, wk_ref, wv_ref, qg_ref, wq_ref, wqp_ref,
     qgain_ref, qgainp_ref, kgain_ref, kgainp_ref, cos_ref, sin_ref,
     q_out, k_out, vt_out, u_out, xp_out) = refs[n_tok:]
    nb_all, tt, d = tok_refs[0].shape
    x_all = _read_tokens(tok_refs, lat_tiles)
    mod_all = mod_ref[...]
    nb = nb_all // PROJ_ROW_SPLITS
    rows = nb * tt
    inv_qk = 1.0 / QK
    tile_rows = lambda a: jnp.concatenate([a] * nb, axis=0)
    q_cos = tile_rows(cos_ref[...] * (qgain_ref[...] * Q_SCALE))
    q_sin = tile_rows(sin_ref[...] * (qgainp_ref[...] * Q_SCALE))
    k_cos = tile_rows(cos_ref[...] * kgain_ref[...])
    k_sin = tile_rows(sin_ref[...] * kgainp_ref[...])

    staged = []
    for s in range(PROJ_ROW_SPLITS):
        bs = slice(s * nb, (s + 1) * nb)
        mod = mod_all[bs]
        shift, scale = mod[:, 0:1, :], mod[:, 1:2, :]
        h = _rms(x_all[bs], ng_ref[...]) * (1.0 + scale) + shift
        z = _dot(h.reshape(rows, d).astype(BF16), w1_ref[...])
        o = 0
        ckv = z[:, o:o + kv_lora]; o += kv_lora
        kr = z[:, o:o + HEAD_PAD]; o += HEAD_PAD
        kp = z[:, o:o + HEAD_PAD]; o += HEAD_PAD
        cq = z[:, o:o + q_lora]; o += q_lora
        u_out[:, bs, :] = jnp.transpose(z[:, o:o + bw].reshape(nb, tt, bw), (1, 0, 2)); o += bw
        xp_out[bs] = z[:, o:o + 2 * bw].reshape(nb, tt, 2 * bw)
        cqn = _rms(cq, qg_ref[...]).astype(BF16)
        ckvn = _rms(ckv, kvg_ref[...]).astype(BF16)
        staged.append((bs, kr, kp, _dot(cqn, wq_ref[...]), _dot(cqn, wqp_ref[...]),
                       _dot(ckvn, wk_ref[...]), _dot(ckvn, wv_ref[...])))

    for bs, kr, kp, qraw, qprt, knope, v in staged:
        for hh in range(HEADS):
            sl = slice(hh * HEAD_PAD, (hh + 1) * HEAD_PAD)
            qh = qraw[:, sl]
            rs = lax.rsqrt(jnp.sum(qh * qh, axis=-1, keepdims=True) * inv_qk + EPS)
            q_out[bs, hh] = ((qh * q_cos + qprt[:, sl] * q_sin) * rs).astype(BF16).reshape(nb, tt, HEAD_PAD)
        kp_sin = kp * k_sin
        for hh in range(HEADS):
            sl = slice(hh * HEAD_PAD, (hh + 1) * HEAD_PAD)
            kh = knope[:, sl] + kr
            rs = lax.rsqrt(jnp.sum(kh * kh, axis=-1, keepdims=True) * inv_qk + EPS)
            k_out[bs, hh] = ((kh * k_cos + kp_sin) * rs).astype(BF16).reshape(nb, tt, HEAD_PAD)
        ones_rows = (lax.broadcasted_iota(jnp.int32, (HEADS, VAUG - VDIM, tt), 1) == 0).astype(BF16)
        for bi in range(nb):
            vt = v[bi * tt:(bi + 1) * tt].T.astype(BF16).reshape(HEADS, VDIM, tt)
            vt_out[bs.start + bi] = jnp.concatenate([vt, ones_rows], axis=1)


def _proj(toks, mod3, w, li, cos_t, sin_t, n_lat):
    b, _, d = toks[0].shape
    t = sum(a.shape[1] for a in toks)
    nb, tt = SUBLANE, PROJ_STEPS
    bw = d // N_BRANCH
    kv_lora, q_lora = w["kv_norm"].shape[-1], w["q_norm"].shape[-1]
    ctx_group = mod3.shape[1] - 1
    lat_tiles = n_lat // tt
    mod_map = lambda i, j: (li, jnp.where(j < lat_tiles, i, ctx_group), 0, 0, 0)
    kern = functools.partial(_proj_kernel, kv_lora=kv_lora, q_lora=q_lora, bw=bw, n_tok=len(toks),
                             lat_tiles=lat_tiles)
    names = ["norm_g", "w1", "kv_norm", "wk", "wv", "q_norm", "wq", "wqp", "qgain", "qgainp", "kgain", "kgainp"]
    return pl.pallas_call(
        kern,
        out_shape=[jax.ShapeDtypeStruct((b, HEADS, t, HEAD_PAD), BF16),
                   jax.ShapeDtypeStruct((b, HEADS, t, HEAD_PAD), BF16),
                   jax.ShapeDtypeStruct((b, HEADS, VAUG, t), BF16),
                   jax.ShapeDtypeStruct((t, b, bw), F32),
                   jax.ShapeDtypeStruct((b, t, 2 * bw), F32)],
        grid=(b // nb, t // tt),
        in_specs=_token_specs(toks, nb, tt, lat_tiles)
                 + [pl.BlockSpec((None, None, nb, 3, d), mod_map)]
                 + [_wspec(w[n], (li,)) for n in names]
                 + [pl.BlockSpec((tt, HEAD_PAD), lambda i, j: (j, 0)),
                    pl.BlockSpec((tt, HEAD_PAD), lambda i, j: (j, 0))],
        out_specs=[pl.BlockSpec((nb, HEADS, tt, HEAD_PAD), lambda i, j: (i, 0, j, 0)),
                   pl.BlockSpec((nb, HEADS, tt, HEAD_PAD), lambda i, j: (i, 0, j, 0)),
                   pl.BlockSpec((nb, HEADS, VAUG, tt), lambda i, j: (i, 0, 0, j)),
                   pl.BlockSpec((tt, nb, bw), lambda i, j: (j, i, 0)),
                   pl.BlockSpec((nb, tt, 2 * bw), lambda i, j: (i, j, 0))],
        compiler_params=_params(("arbitrary", "arbitrary")),
        name="proj_qkv",
    )(*toks, mod3, *[w[n] for n in names], cos_t, sin_t)


def _seq_kernel(xp_ref, cw_ref, cb_ref, pw_ref, pb_ref, ps_ref, xl_out, op_out,
                pad_ref, f2_ref, f4_ref, f8_ref, inv_ref, *, segs, bw):
    assert POOL_WINDOWS == (2, 4, 8, 16) and bw == 2 * LANE and HALO == 8
    gw = bw // len(POOL_WINDOWS)
    cw = cw_ref[...]
    left = LRU_CONV // 2

    @pl.when(pl.program_id(0) == 0)
    def _():
        for lo, t in segs:
            lane = lax.broadcasted_iota(jnp.int32, (t, bw), 1)
            row = lax.broadcasted_iota(jnp.int32, (t, bw), 0)
            half = jnp.ones((t, bw), jnp.int32)
            for gi in range(1, len(POOL_WINDOWS)):
                half = jnp.where(lane >= gi * gw, POOL_WINDOWS[gi] // 2, half)
            cnt = (jnp.minimum(row + half, t) - jnp.maximum(row - half, 0)).astype(F32)
            inv_ref[lo:lo + t, :] = 1.0 / cnt

    lane_half = lax.broadcasted_iota(jnp.int32, (1, LANE), 1) < gw
    for lo, t in segs:
        n = t + 2 * HALO
        pad_ref[0:HALO, :] = jnp.zeros((HALO, 2 * bw), F32)
        pad_ref[HALO + t:n + HALO, :] = jnp.zeros((2 * HALO, 2 * bw), F32)
        pad_ref[HALO:HALO + t, :] = xp_ref[0, lo:lo + t, :]
        f2_ref[n:n + HALO, :] = jnp.zeros((HALO, bw), F32)
        f4_ref[n:n + HALO, :] = jnp.zeros((HALO, bw), F32)
        f8_ref[n:n + HALO, :] = jnp.zeros((HALO, LANE), F32)

        acc = cb_ref[...] + pad_ref[HALO - left:HALO - left + t, 0:bw] * cw[0:1]
        for k in range(1, LRU_CONV):
            acc = acc + pad_ref[HALO + k - left:HALO + k - left + t, 0:bw] * cw[k:k + 1]
        xl_out[0, lo:lo + t, :] = acc

        f2_ref[0:n, :] = pad_ref[0:n, bw:2 * bw] + pad_ref[1:n + 1, bw:2 * bw]
        f4_ref[0:n, :] = f2_ref[0:n, :] + f2_ref[2:n + 2, :]
        f8_ref[0:n, :] = f4_ref[0:n, LANE:bw] + f4_ref[4:n + 4, LANE:bw]
        sel_lo = jnp.where(lane_half, f2_ref[7:7 + t, 0:LANE], f4_ref[6:6 + t, 0:LANE])
        sel_hi = jnp.where(lane_half, f8_ref[4:4 + t, :], f8_ref[0:t, :] + f8_ref[HALO:HALO + t, :])
        sel = jnp.concatenate([sel_lo, sel_hi], axis=1)
        p = sel * inv_ref[lo:lo + t, :] - pad_ref[HALO:HALO + t, bw:2 * bw]
        y = _dot(p.astype(BF16), pw_ref[...]) + pb_ref[...]
        op_out[0, lo:lo + t, :] = y * ps_ref[...]


def _seq(xp, w, li, segs):
    b, t, w2 = xp.shape
    bw = w2 // 2
    kern = functools.partial(_seq_kernel, segs=segs, bw=bw)
    names = ["conv_w", "conv_b", "pool_w", "pool_b", "pool_scale"]
    max_t = max(n for _, n in segs)
    return pl.pallas_call(
        kern,
        out_shape=[jax.ShapeDtypeStruct((b, t, bw), F32), jax.ShapeDtypeStruct((b, t, bw), F32)],
        grid=(b,),
        in_specs=[pl.BlockSpec((1, t, w2), lambda i: (i, 0, 0))] + [_wspec(w[n], (li,)) for n in names],
        out_specs=[pl.BlockSpec((1, t, bw), lambda i: (i, 0, 0)),
                   pl.BlockSpec((1, t, bw), lambda i: (i, 0, 0))],
        scratch_shapes=[pltpu.VMEM((max_t + 3 * HALO, w2), F32), pltpu.VMEM((max_t + 3 * HALO, bw), F32),
                        pltpu.VMEM((max_t + 3 * HALO, bw), F32), pltpu.VMEM((max_t + 3 * HALO, LANE), F32),
                        pltpu.VMEM((t, bw), F32)],
        compiler_params=_params(("arbitrary",)),
        name="conv_pool",
    )(xp, *[w[n] for n in names])


def _attn_kernel(q_ref, k_ref, vt_ref, o_ref):
    n_k = k_ref.shape[2]
    ck = min(KEY_CHUNK, n_k)
    nch = n_k // ck
    items = [(hh, c) for hh in range(HEADS) for c in range(nch)]

    def scores(item):
        hh, c = item
        return _dot_nt(k_ref[0, hh, c * ck:(c + 1) * ck, :], q_ref[0, hh])

    outs = []
    m = acc = None
    ahead = [scores(it) for it in items[:ATTN_LOOKAHEAD]]
    for i, (hh, c) in enumerate(items):
        sc = ahead.pop(0)
        if i + ATTN_LOOKAHEAD < len(items):
            ahead.append(scores(items[i + ATTN_LOOKAHEAD]))
        ks = slice(c * ck, (c + 1) * ck)
        mc = jnp.max(sc, axis=0, keepdims=True)
        if c == 0:
            m_new = mc
            acc = _dot(vt_ref[0, hh, :, ks], jnp.exp2(sc - m_new).astype(BF16))
        else:
            m_new = jnp.maximum(m, mc)
            acc = jnp.exp2(m - m_new) * acc + _dot(vt_ref[0, hh, :, ks], jnp.exp2(sc - m_new).astype(BF16))
        m = m_new
        if c == nch - 1:
            outs.append(acc[0:VDIM] / acc[VDIM:VDIM + 1])
    o_ref[0] = jnp.concatenate(outs, axis=0).T


def _attend(q, k, vt, n_q, q_off, n_k, k_off, tq):
    b = q.shape[0]
    hv = HEADS * VDIM
    assert n_q % tq == 0 and q_off % tq == 0 and k_off % n_k == 0
    qo, ko = q_off // tq, k_off // n_k
    return pl.pallas_call(
        _attn_kernel,
        out_shape=jax.ShapeDtypeStruct((b, n_q, hv), F32),
        grid=(b, n_q // tq),
        in_specs=[pl.BlockSpec((1, HEADS, tq, HEAD_PAD), lambda i, j: (i, 0, j + qo, 0)),
                  pl.BlockSpec((1, HEADS, n_k, HEAD_PAD), lambda i, j: (i, 0, ko, 0)),
                  pl.BlockSpec((1, HEADS, VAUG, n_k), lambda i, j: (i, 0, 0, ko))],
        out_specs=pl.BlockSpec((1, tq, hv), lambda i, j: (i, j, 0)),
        compiler_params=_params(("arbitrary", "arbitrary")),
        name="attention",
    )(q, k, vt)


def _scan_kernel(uf_ref, ub_ref, xf_ref, xb_ref, bmat_ref, cre_ref, cim_ref, ar_ref, ai_ref,
                 wa_ref, ba_ref, wx_ref, bx_ref, lam_ref, yf_ref, yb_ref,
                 hbuf, abuf, bbuf, hr_s, hi_s, hl_s, *, tc, bsz, bw, ns):
    r = tc * bsz
    sb = SCAN_SUB
    nsub = tc // sb
    rs = sb * bsz
    nblk = bw // LANE
    sblk = ns // nblk
    u_refs, x_refs, y_refs = (uf_ref, ub_ref), (xf_ref, xb_ref), (yf_ref, yb_ref)

    @pl.when(pl.program_id(0) == 0)
    def _():
        hr_s[...] = jnp.zeros_like(hr_s)
        hi_s[...] = jnp.zeros_like(hi_s)
        hl_s[...] = jnp.zeros_like(hl_s)

    u16 = [u_refs[d][...].reshape(r, bw).astype(BF16) for d in range(2)]
    xls = [jnp.transpose(x_refs[d][...], (1, 0, 2)).reshape(r, bw) for d in range(2)]
    softplus = []
    for d in range(2):
        nl = -lam_ref[d]
        softplus.append(jnp.maximum(nl, 0.0) + jnp.log1p(jnp.exp(-jnp.abs(nl))))

    def sub_rows(d, k):
        kk = (nsub - 1 - k) if d else k
        return kk, slice(kk * rs, (kk + 1) * rs)

    def input_stage(k):
        for d in range(2):
            _, rw = sub_rows(d, k)
            hbuf[d, rw, :] = _dot(u16[d][rw], bmat_ref[d])

    def gate_stage(k):
        for d in range(2):
            _, rw = sub_rows(d, k)
            xl = xls[d][rw]
            xb = xl.astype(BF16)
            rg = jax.nn.sigmoid(_dot(xb, wa_ref[d]) + ba_ref[d])
            ig = jax.nn.sigmoid(_dot(xb, wx_ref[d]) + bx_ref[d])
            log_a = (-LRU_C) * rg * softplus[d]
            abuf[d, rw, :] = jnp.exp(log_a)
            th = jnp.tanh(log_a)
            gain2 = -2.0 * th / (1.0 - th)
            gain = jnp.where(gain2 > 0.0, gain2 * lax.rsqrt(gain2), 0.0)
            bbuf[d, rw, :] = gain * (ig * xl)

    w = ns // SCAN_LANE_SPLITS
    lanes = [(slice(j * w, (j + 1) * w), slice(ns + j * w, ns + (j + 1) * w)) for j in range(SCAN_LANE_SPLITS)]
    coef = [[(jnp.broadcast_to(ar_ref[d, :, re_sl], (bsz, w)), jnp.broadcast_to(ai_ref[d, :, re_sl], (bsz, w)))
             for re_sl, _ in lanes] for d in range(2)]
    hl = [hl_s[d] for d in range(2)]

    def scan_stage(k):
        def step_rows(t, d):
            te = (tc - 1 - t) if d else t
            return te, slice(te * bsz, (te + 1) * bsz)

        for t in range(k * sb, (k + 1) * sb):
            for d in range(2):
                te, rw = step_rows(t, d)
                hl[d] = abuf[d, rw, :] * hl[d] + bbuf[d, rw, :]
                y_refs[d][te, :, bw:2 * bw] = hl[d]
        for j, (re_sl, im_sl) in enumerate(lanes):
            for t in range(k * sb, (k + 1) * sb):
                for d in range(2):
                    _, rw = step_rows(t, d)
                    if t == 0:
                        phr, phi = hr_s[d, :, re_sl], hi_s[d, :, re_sl]
                    else:
                        _, prw = step_rows(t - 1, d)
                        phr, phi = hbuf[d, prw, re_sl], hbuf[d, prw, im_sl]
                    ar, ai = coef[d][j]
                    hbuf[d, rw, re_sl] = ar * phr - ai * phi + hbuf[d, rw, re_sl]
                    hbuf[d, rw, im_sl] = ar * phi + ai * phr + hbuf[d, rw, im_sl]

    def readout_stage(k):
        for d in range(2):
            kk, rw = sub_rows(d, k)
            for jb in range(nblk):
                st = slice(jb * sblk, (jb + 1) * sblk)
                sti = slice(ns + jb * sblk, ns + (jb + 1) * sblk)
                ln = slice(jb * LANE, (jb + 1) * LANE)
                y = (_dot(hbuf[d, rw, st].astype(BF16), cre_ref[d, st, ln])
                     - _dot(hbuf[d, rw, sti].astype(BF16), cim_ref[d, st, ln]))
                y_refs[d][kk * sb:(kk + 1) * sb, :, ln] = y.reshape(sb, bsz, LANE)

    lead = min(2, nsub)
    for k in range(lead):
        input_stage(k)
    for k in range(lead):
        gate_stage(k)
    for k in range(nsub):
        if k + lead < nsub:
            input_stage(k + lead)
        scan_stage(k)
        if k >= 1:
            readout_stage(k - 1)
        if k + lead < nsub:
            gate_stage(k + lead)
    readout_stage(nsub - 1)

    for d in range(2):
        hl_s[d] = hl[d]
        last = slice((0 if d else tc - 1) * bsz, (1 if d else tc) * bsz)
        hr_s[d] = hbuf[d, last, 0:ns]
        hi_s[d] = hbuf[d, last, ns:2 * ns]


def _scan(u_tm, xl, w, li, n_lat):
    t, bsz, bw = u_tm.shape
    tc = SCAN_CHUNK
    nc = t // tc
    ncl = n_lat // tc
    ncc = nc - ncl
    ns = w["ar"].shape[-1]
    fwd = lambda s: jnp.where(s < ncc, ncl + s, s - ncc)
    bwd = lambda s: nc - 1 - s
    names = ["bmat", "cre", "cim", "ar", "ai", "wa", "ba", "wx", "bx", "lam"]
    r = tc * bsz
    kern = functools.partial(_scan_kernel, tc=tc, bsz=bsz, bw=bw, ns=ns)
    return pl.pallas_call(
        kern,
        out_shape=[jax.ShapeDtypeStruct((t, bsz, 2 * bw), F32)] * 2,
        grid=(nc,),
        in_specs=[pl.BlockSpec((tc, bsz, bw), lambda s: (fwd(s), 0, 0)),
                  pl.BlockSpec((tc, bsz, bw), lambda s: (bwd(s), 0, 0)),
                  pl.BlockSpec((bsz, tc, bw), lambda s: (0, fwd(s), 0)),
                  pl.BlockSpec((bsz, tc, bw), lambda s: (0, bwd(s), 0))]
                 + [_wspec(w[n], (li,)) for n in names],
        out_specs=[pl.BlockSpec((tc, bsz, 2 * bw), lambda s: (fwd(s), 0, 0)),
                   pl.BlockSpec((tc, bsz, 2 * bw), lambda s: (bwd(s), 0, 0))],
        scratch_shapes=[pltpu.VMEM((2, r, 2 * ns), F32), pltpu.VMEM((2, r, bw), F32), pltpu.VMEM((2, r, bw), F32),
                        pltpu.VMEM((2, bsz, ns), F32), pltpu.VMEM((2, bsz, ns), F32),
                        pltpu.VMEM((2, bsz, bw), F32)],
        compiler_params=_params(("arbitrary",)),
        name="scan_bidir",
    )(u_tm, u_tm, xl, xl, *[w[n] for n in names])


def _merge_kernel(*refs, bw, lat_tiles, with_ctx, n_tok):
    tok_refs, refs = refs[:n_tok], refs[n_tok:]
    if with_ctx:
        (mod_ref, ng_ref, om_ref, omc_ref, yf_ref, yb_ref, u_ref, op_ref,
         w2_ref, wb_ref, wo_ref, wglu_ref, d_ref, out_ref) = refs
    else:
        (mod_ref, ng_ref, om_ref, yf_ref, yb_ref, u_ref, op_ref,
         w2_ref, wb_ref, wo_ref, wglu_ref, d_ref, out_ref) = refs
    nb_all, tt, d = tok_refs[0].shape
    x_all = _read_tokens(tok_refs, lat_tiles)
    mod_all = mod_ref[...]
    rr_all = jnp.transpose(yf_ref[...] + yb_ref[...], (1, 0, 2))
    u_all = jnp.transpose(u_ref[...], (1, 0, 2))
    o_mla_all = om_ref[...]
    if with_ctx:
        o_mla_all = jnp.where(pl.program_id(1) >= lat_tiles, omc_ref[...], o_mla_all)
    op_all = op_ref[...]

    nb = nb_all // MERGE_ROW_SPLITS
    rows = nb * tt
    xs, hs, gates, branches = [], [], [], []
    for s in range(MERGE_ROW_SPLITS):
        bs = slice(s * nb, (s + 1) * nb)
        x = x_all[bs]
        mod = mod_all[bs]
        shift, scale, gate = mod[:, 0:1, :], mod[:, 1:2, :], mod[:, 2:3, :]
        h = (_rms(x, ng_ref[...]) * (1.0 + scale) + shift).reshape(rows, d).astype(BF16)
        rr = rr_all[bs].reshape(rows, 2 * bw)
        u = u_all[bs].reshape(rows, bw)
        g = jax.nn.gelu(rr[:, 0:bw] + d_ref[...] * u, approximate=True)
        o_s5 = g * jax.nn.sigmoid(_dot(g.astype(BF16), wglu_ref[...]))
        xs.append(x)
        hs.append(h)
        gates.append(gate)
        branches.append((o_mla_all[bs].reshape(rows, bw), o_s5, rr[:, bw:2 * bw], op_all[bs].reshape(rows, bw)))

    ys = [None] * MERGE_ROW_SPLITS
    for n in range(N_BRANCH):
        for s in range(MERGE_ROW_SPLITS):
            gp = _dot(hs[s], w2_ref[:, n * bw:(n + 1) * bw])
            off = N_BRANCH * bw + n * d
            ml = _dot(hs[s], w2_ref[:, off:off + d])
            tn = _dot((branches[s][n] * (gp * jax.nn.sigmoid(gp))).astype(BF16), wb_ref[n])
            term = jax.nn.sigmoid(ml) * tn
            ys[s] = term if ys[s] is None else ys[s] + term
    for s in range(MERGE_ROW_SPLITS):
        bs = slice(s * nb, (s + 1) * nb)
        out_ref[bs] = xs[s] + gates[s] * _dot(ys[s].astype(BF16), wo_ref[...]).reshape(nb, tt, d)


def _merge(toks, mod3, o_lat, o_ctx, yf, yb, u_tm, o_pool, w, li, n_lat, n_out):
    b, _, d = toks[0].shape
    nb, tt = SUBLANE, MERGE_STEPS
    bw = d // N_BRANCH
    with_ctx = o_ctx is not None
    lat_tiles = n_lat // tt
    ctx_group = mod3.shape[1] - 1
    mod_map = lambda i, j: (li, jnp.where(j < lat_tiles, i, ctx_group), 0, 0, 0)
    bm = lambda width: pl.BlockSpec((nb, tt, width), lambda i, j: (i, j, 0))
    tm = lambda width: pl.BlockSpec((tt, nb, width), lambda i, j: (j, i, 0))
    names = ["w2", "wb", "wo", "wglu", "s5_d"]
    in_specs = _token_specs(toks, nb, tt, lat_tiles) + [
        pl.BlockSpec((None, None, nb, 3, d), mod_map), _wspec(w["norm_g"], (li,)),
        pl.BlockSpec((nb, tt, bw), lambda i, j: (i, jnp.minimum(j, lat_tiles - 1), 0))]
    args = [*toks, mod3, w["norm_g"], o_lat]
    if with_ctx:
        ctx_tiles = o_ctx.shape[1] // tt
        in_specs.append(pl.BlockSpec((nb, tt, bw), lambda i, j: (i, jnp.clip(j - lat_tiles, 0, ctx_tiles - 1), 0)))
        args.append(o_ctx)
    in_specs += [tm(2 * bw), tm(2 * bw), tm(bw), bm(bw)] + [_wspec(w[n], (li,)) for n in names]
    args += [yf, yb, u_tm, o_pool] + [w[n] for n in names]
    kern = functools.partial(_merge_kernel, bw=bw, lat_tiles=lat_tiles, with_ctx=with_ctx, n_tok=len(toks))
    return pl.pallas_call(
        kern,
        out_shape=jax.ShapeDtypeStruct((b, n_out, d), F32),
        grid=(b // nb, n_out // tt),
        in_specs=in_specs,
        out_specs=bm(d),
        compiler_params=_params(("arbitrary", "arbitrary")),
        name="merge_out",
    )(*args)


def _block_diag(blocks):
    n, a, b = blocks.shape[-3:]
    eye = jnp.eye(n, dtype=blocks.dtype)
    return jnp.einsum("...nab,nm->...namb", blocks, eye).reshape(blocks.shape[:-3] + (n * a, n * b))


def _rope_perm():
    half = ROPE // 2
    nf = half // 2
    base = jnp.concatenate([jnp.arange(nf, half), jnp.arange(0, nf)])
    return jnp.concatenate([base, base + half])


def _head_pad(a, lo):
    n = a.shape[-1]
    pad = [(0, 0)] * (a.ndim - 1) + [(lo, HEAD_PAD - lo - n)]
    return jnp.pad(a, pad)


def _rope_tables(n_lat, n_ctx):
    rows_n = n_lat // GRID_W
    row = jnp.repeat(jnp.arange(rows_n, dtype=jnp.int32), GRID_W).astype(F32)
    col = jnp.tile(jnp.arange(GRID_W, dtype=jnp.int32), rows_n).astype(F32)
    nf = ROPE // 4
    inv = ROPE_THETA ** (-jnp.arange(nf, dtype=F32) / nf)
    ang_r = row[:, None] * inv
    ang_c = col[:, None] * inv
    cr, sr, cc, sc = jnp.cos(ang_r), jnp.sin(ang_r), jnp.cos(ang_c), jnp.sin(ang_c)
    ones = jnp.ones((n_lat, NOPE), F32)
    zeros = jnp.zeros((n_lat, NOPE), F32)
    tail = HEAD_PAD - QK
    cos = jnp.concatenate([ones, cr, cr, cc, cc, jnp.ones((n_lat, tail), F32)], -1)
    sin = jnp.concatenate([zeros, -sr, sr, -sc, sc, jnp.zeros((n_lat, tail), F32)], -1)
    cos = jnp.concatenate([cos, jnp.ones((n_ctx, HEAD_PAD), F32)], 0)
    sin = jnp.concatenate([sin, jnp.zeros((n_ctx, HEAD_PAD), F32)], 0)
    return cos, sin


def _prep_weights(p, disc):
    w_in = p["w_in"]
    depth, d, _ = w_in.shape
    bw = d // N_BRANCH
    kv_lora = p["mla_kv_norm"].shape[-1]
    q_lora = p["mla_q_norm"].shape[-1]
    perm = _rope_perm()
    off_krope = kv_lora
    off_s5 = off_krope + ROPE
    off_lru = off_s5 + bw
    off_cq = off_lru + bw
    off_pool = off_cq + q_lora
    off_gate = off_pool + bw
    w_mix = w_in[:, :, :off_gate].astype(BF16)
    w_krope = w_mix[:, :, off_krope:off_s5]
    w1 = jnp.concatenate([
        w_mix[:, :, :kv_lora],
        _head_pad(w_krope, NOPE),
        _head_pad(w_krope[:, :, perm], NOPE),
        w_mix[:, :, off_cq:off_pool],
        w_mix[:, :, off_s5:off_lru],
        w_mix[:, :, off_lru:off_cq],
        w_mix[:, :, off_pool:off_gate]], axis=2)
    w2 = w_in[:, :, off_gate:].astype(BF16)

    w_uq = p["mla_w_uq"].reshape(depth, q_lora, HEADS, QK)
    wq = _head_pad(w_uq, 0).reshape(depth, q_lora, HEADS * HEAD_PAD).astype(BF16)
    wqp = _head_pad(w_uq[..., NOPE:][..., perm], NOPE).reshape(depth, q_lora, HEADS * HEAD_PAD).astype(BF16)
    w_ukv = p["mla_w_ukv"].reshape(depth, kv_lora, HEADS, NOPE + VDIM)
    wk = _head_pad(w_ukv[..., :NOPE], 0).reshape(depth, kv_lora, HEADS * HEAD_PAD).astype(BF16)
    wv = w_ukv[..., NOPE:].reshape(depth, kv_lora, HEADS * VDIM).astype(BF16)
    q_gain, k_gain = p["mla_q_gain"], p["mla_k_gain"]
    row = lambda a: a[..., None, :]

    ab_re, ab_im, bb_re, bb_im = disc
    sw = lambda a: jnp.swapaxes(a, -1, -2)
    bmat = jnp.concatenate([_block_diag(sw(bb_re)), _block_diag(sw(bb_im))], axis=-1).astype(BF16)
    return dict(
        norm_g=row(p["norm_g"]), w1=w1, w2=w2,
        kv_norm=row(p["mla_kv_norm"]), q_norm=row(p["mla_q_norm"]),
        wq=wq, wqp=wqp, wk=wk, wv=wv,
        qgain=row(_head_pad(q_gain, 0)), qgainp=row(_head_pad(q_gain[:, NOPE:][:, perm], NOPE)),
        kgain=row(_head_pad(k_gain, 0)), kgainp=row(_head_pad(k_gain[:, NOPE:][:, perm], NOPE)),
        conv_w=p["lru_conv_w"], conv_b=row(p["lru_conv_b"]),
        pool_w=_block_diag(p["pool_w"]).astype(BF16), pool_b=row(p["pool_b"]),
        pool_scale=row(p["pool_scale"]),
        wb=p["w_branch"].astype(BF16), wo=p["w_out"].astype(BF16),
        wglu=p["s5_w_glu"].astype(BF16), s5_d=row(p["s5_d"]),
        bmat=bmat,
        cre=_block_diag(sw(p["s5_c_re"])).astype(BF16), cim=_block_diag(sw(p["s5_c_im"])).astype(BF16),
        ar=row(ab_re.reshape(ab_re.shape[:2] + (-1,))), ai=row(ab_im.reshape(ab_im.shape[:2] + (-1,))),
        wa=_block_diag(p["lru_w_a"]).astype(BF16), ba=row(p["lru_b_a"]),
        wx=_block_diag(p["lru_w_x"]).astype(BF16), bx=row(p["lru_b_x"]),
        lam=row(p["lru_lambda"]))


def kernel(x, c, ctx, c_ctx, w_ada, b_ada, norm_g, w_in, mla_q_norm, mla_kv_norm, mla_w_uq, mla_w_ukv, mla_q_gain, mla_k_gain, s5_a_re, s5_a_im, s5_log_dt, s5_b_re, s5_b_im, s5_c_re, s5_c_im, s5_d, s5_w_glu, lru_conv_w, lru_conv_b, lru_lambda, lru_w_a, lru_b_a, lru_w_x, lru_b_x, pool_w, pool_b, pool_scale, w_branch, w_out):
    p = dict(w_in=w_in, norm_g=norm_g, mla_q_norm=mla_q_norm, mla_kv_norm=mla_kv_norm,
             mla_w_uq=mla_w_uq, mla_w_ukv=mla_w_ukv, mla_q_gain=mla_q_gain, mla_k_gain=mla_k_gain,
             s5_c_re=s5_c_re, s5_c_im=s5_c_im, s5_d=s5_d, s5_w_glu=s5_w_glu,
             lru_conv_w=lru_conv_w, lru_conv_b=lru_conv_b, lru_lambda=lru_lambda,
             lru_w_a=lru_w_a, lru_b_a=lru_b_a, lru_w_x=lru_w_x, lru_b_x=lru_b_x,
             pool_w=pool_w, pool_b=pool_b, pool_scale=pool_scale, w_branch=w_branch, w_out=w_out)
    depth = w_in.shape[0]
    b, l, d = x.shape
    lc = ctx.shape[1]
    t = l + lc
    assert b % SUBLANE == 0 and l % PROJ_STEPS == 0 and lc % PROJ_STEPS == 0 and l % lc == 0

    n_rows = -(-(b + 1) // SUBLANE) * SUBLANE
    cc = jnp.concatenate([c, c_ctx[None], jnp.zeros((n_rows - b - 1, d), F32)], axis=0)
    mod_all = _ada(cc, w_ada, b_ada)
    mod3 = jnp.concatenate([
        mod_all[:, :b].reshape(depth, b // SUBLANE, SUBLANE, 3, d),
        jnp.broadcast_to(mod_all[:, b:b + 1].reshape(depth, 1, 1, 3, d), (depth, 1, SUBLANE, 3, d))], axis=1)
    disc = _s5_discretize(s5_a_re, s5_a_im, s5_log_dt, s5_b_re, s5_b_im)
    w = _prep_weights(p, disc)
    cos_t, sin_t = _rope_tables(l, lc)
    tq_l, tq_c = min(Q_TILE, l), min(Q_TILE, lc)

    toks = (x, ctx)
    for li in range(depth):
        with_ctx = li < depth - 1
        q, k, vt, u_tm, xp = _proj(toks, mod3, w, li, cos_t, sin_t, l)
        xl, o_pool = _seq(xp, w, li, ((0, l), (l, lc)))
        o_lat = _attend(q, k, vt, l, 0, t, 0, tq_l)
        o_ctx = _attend(q, k, vt, lc, l, lc, l, tq_c) if with_ctx else None
        yf, yb = _scan(u_tm, xl, w, li, l)
        toks = (_merge(toks, mod3, o_lat, o_ctx, yf, yb, u_tm, o_pool, w, li, l, t if with_ctx else l),)
    return toks[0]
```

```python
import functools
import math

import jax
import jax.numpy as jnp
from jax import lax
from jax.experimental import pallas as pl
from jax.experimental.pallas import tpu as pltpu

F32 = jnp.float32
BF16 = jnp.bfloat16

EPS = 1e-6
GRID_W = 64
N_BRANCH = 4
HEADS = 4
NOPE = 64
ROPE = 32
VDIM = 64
VAUG = 80
QK = NOPE + ROPE
ROPE_THETA = 10000.0
LRU_CONV = 4
LRU_C = 8.0
POOL_WINDOWS = (2, 4, 8, 16)
Q_SCALE = (QK ** -0.5) * math.log2(math.e)

LANE = 128
SUBLANE = 8
HEAD_PAD = LANE
VMEM_LIMIT = 56 * 1024 * 1024

PROJ_STEPS = 128
PROJ_ROW_SPLITS = 2
MERGE_STEPS = 64
MERGE_ROW_SPLITS = 2
Q_TILE = 512
KEY_CHUNK = 256
ATTN_LOOKAHEAD = 2
SCAN_CHUNK = 64
SCAN_SUB = 32
SCAN_LANE_SPLITS = 2
HALO = 8


def _dot(a, b):
    return jnp.dot(a, b, preferred_element_type=F32)


def _dot_nt(a, b):
    return lax.dot_general(a, b, (((1,), (1,)), ((), ())), preferred_element_type=F32)


def _rms(x, g):
    return x * lax.rsqrt(jnp.mean(x * x, axis=-1, keepdims=True) + EPS) * g


def _wspec(arr, lead):
    tail = arr.shape[len(lead):]
    idx = tuple(lead) + (0,) * len(tail)
    return pl.BlockSpec((None,) * len(lead) + tail, lambda *_: idx, pipeline_mode=pl.Buffered(1))


def _token_specs(toks, nb, tt, lat_tiles):
    if len(toks) == 1:
        return [pl.BlockSpec((nb, tt, toks[0].shape[2]), lambda i, j: (i, j, 0))]
    lat, ctx = toks
    ctx_tiles = ctx.shape[1] // tt
    return [pl.BlockSpec((nb, tt, lat.shape[2]), lambda i, j: (i, jnp.minimum(j, lat_tiles - 1), 0)),
            pl.BlockSpec((nb, tt, ctx.shape[2]), lambda i, j: (i, jnp.clip(j - lat_tiles, 0, ctx_tiles - 1), 0))]


def _read_tokens(tok_refs, lat_tiles):
    if len(tok_refs) == 1:
        return tok_refs[0][...]
    return jnp.where(pl.program_id(1) >= lat_tiles, tok_refs[1][...], tok_refs[0][...])


def _params(sem):
    return pltpu.CompilerParams(dimension_semantics=sem, vmem_limit_bytes=VMEM_LIMIT)


def _ada_kernel(c_ref, w_ref, b_ref, o_ref):
    c = c_ref[...]
    act = c * jax.nn.sigmoid(c)
    o_ref[0] = jnp.dot(act, w_ref[0], preferred_element_type=F32,
                       precision=lax.Precision.HIGHEST) + b_ref[0]


def _ada(cc, w_ada, b_ada):
    depth, d, d3 = w_ada.shape
    rows = cc.shape[0]
    col = d
    return pl.pallas_call(
        _ada_kernel,
        out_shape=jax.ShapeDtypeStruct((depth, rows, d3), F32),
        grid=(depth, d3 // col),
        in_specs=[pl.BlockSpec((rows, d), lambda l, j: (0, 0)),
                  pl.BlockSpec((1, d, col), lambda l, j: (l, 0, j)),
                  pl.BlockSpec((1, 1, col), lambda l, j: (l, 0, j))],
        out_specs=pl.BlockSpec((1, rows, col), lambda l, j: (l, 0, j)),
        compiler_params=_params(("arbitrary", "arbitrary")),
        name="ada_mod",
    )(cc, w_ada, b_ada.reshape(depth, 1, d3))


def _s5_disc_kernel(are_ref, aim_ref, ldt_ref, bre_ref, bim_ref, abre_ref, abim_ref, bbre_ref, bbim_ref):
    a_re = are_ref[...]
    a_im = aim_ref[...]
    dt = jnp.exp(ldt_ref[...])
    mag = jnp.exp(a_re * dt)
    ab_re = mag * jnp.cos(a_im * dt)
    ab_im = mag * jnp.sin(a_im * dt)
    den = a_re * a_re + a_im * a_im
    f_re = ((ab_re - 1.0) * a_re + ab_im * a_im) / den
    f_im = (ab_im * a_re - (ab_re - 1.0) * a_im) / den
    b_re = bre_ref[...]
    b_im = bim_ref[...]
    abre_ref[...] = ab_re
    abim_ref[...] = ab_im
    bbre_ref[...] = f_re * b_re - f_im * b_im
    bbim_ref[...] = f_re * b_im + f_im * b_re


def _s5_discretize(a_re, a_im, log_dt, b_re, b_im):
    full = b_re.shape
    rows = b_re.size // LANE
    expand = lambda a: jnp.broadcast_to(a[..., None], full).reshape(rows, LANE)
    ins = [expand(a_re), expand(a_im), expand(jnp.broadcast_to(log_dt[..., None], a_re.shape)),
           b_re.reshape(rows, LANE), b_im.reshape(rows, LANE)]
    outs = pl.pallas_call(
        _s5_disc_kernel,
        out_shape=[jax.ShapeDtypeStruct((rows, LANE), F32)] * 4,
        name="s5_discretize",
    )(*ins)
    ab_re, ab_im, bb_re, bb_im = [o.reshape(full) for o in outs]
    return ab_re[..., 0], ab_im[..., 0], bb_re, bb_im


def _proj_kernel(*refs, kv_lora, q_lora, bw, n_tok, lat_tiles):
    tok_refs = refs[:n_tok]
    (mod_ref, ng_ref, w1_ref, kvg_ref, wk_ref, wv_ref, qg_ref, wq_ref, wqp_ref,
     qgain_ref, qgainp_ref, kgain_ref, kgainp_ref, cos_ref, sin_ref,
     q_out, k_out, vt_out, u_out, xp_out) = refs[n_tok:]
    nb_all, tt, d = tok_refs[0].shape
    x_all = _read_tokens(tok_refs, lat_tiles)
    mod_all = mod_ref[...]
    nb = nb_all // PROJ_ROW_SPLITS
    rows = nb * tt
    inv_qk = 1.0 / QK
    tile_rows = lambda a: jnp.concatenate([a] * nb, axis=0)
    q_cos = tile_rows(cos_ref[...] * (qgain_ref[...] * Q_SCALE))
    q_sin = tile_rows(sin_ref[...] * (qgainp_ref[...] * Q_SCALE))
    k_cos = tile_rows(cos_ref[...] * kgain_ref[...])
    k_sin = tile_rows(sin_ref[...] * kgainp_ref[...])

    staged = []
    for s in range(PROJ_ROW_SPLITS):
        bs = slice(s * nb, (s + 1) * nb)
        mod = mod_all[bs]
        shift, scale = mod[:, 0:1, :], mod[:, 1:2, :]
        h = _rms(x_all[bs], ng_ref[...]) * (1.0 + scale) + shift
        z = _dot(h.reshape(rows, d).astype(BF16), w1_ref[...])
        o = 0
        ckv = z[:, o:o + kv_lora]; o += kv_lora
        kr = z[:, o:o + HEAD_PAD]; o += HEAD_PAD
        kp = z[:, o:o + HEAD_PAD]; o += HEAD_PAD
        cq = z[:, o:o + q_lora]; o += q_lora
        u_out[:, bs, :] = jnp.transpose(z[:, o:o + bw].reshape(nb, tt, bw), (1, 0, 2)); o += bw
        xp_out[bs] = z[:, o:o + 2 * bw].reshape(nb, tt, 2 * bw)
        cqn = _rms(cq, qg_ref[...]).astype(BF16)
        ckvn = _rms(ckv, kvg_ref[...]).astype(BF16)
        staged.append((bs, kr, kp, _dot(cqn, wq_ref[...]), _dot(cqn, wqp_ref[...]),
                       _dot(ckvn, wk_ref[...]), _dot(ckvn, wv_ref[...])))

    for bs, kr, kp, qraw, qprt, knope, v in staged:
        for hh in range(HEADS):
            sl = slice(hh * HEAD_PAD, (hh + 1) * HEAD_PAD)
            qh = qraw[:, sl]
            rs = lax.rsqrt(jnp.sum(qh * qh, axis=-1, keepdims=True) * inv_qk + EPS)
            q_out[bs, hh] = ((qh * q_cos + qprt[:, sl] * q_sin) * rs).astype(BF16).reshape(nb, tt, HEAD_PAD)
        kp_sin = kp * k_sin
        for hh in range(HEADS):
            sl = slice(hh * HEAD_PAD, (hh + 1) * HEAD_PAD)
            kh = knope[:, sl] + kr
            rs = lax.rsqrt(jnp.sum(kh * kh, axis=-1, keepdims=True) * inv_qk + EPS)
            k_out[bs, hh] = ((kh * k_cos + kp_sin) * rs).astype(BF16).reshape(nb, tt, HEAD_PAD)
        ones_rows = (lax.broadcasted_iota(jnp.int32, (HEADS, VAUG - VDIM, tt), 1) == 0).astype(BF16)
        for bi in range(nb):
            vt = v[bi * tt:(bi + 1) * tt].T.astype(BF16).reshape(HEADS, VDIM, tt)
            vt_out[bs.start + bi] = jnp.concatenate([vt, ones_rows], axis=1)


def _proj(toks, mod3, w, li, cos_t, sin_t, n_lat):
    b, _, d = toks[0].shape
    t = sum(a.shape[1] for a in toks)
    nb, tt = SUBLANE, PROJ_STEPS
    bw = d // N_BRANCH
    kv_lora, q_lora = w["kv_norm"].shape[-1], w["q_norm"].shape[-1]
    ctx_group = mod3.shape[1] - 1
    lat_tiles = n_lat // tt
    mod_map = lambda i, j: (li, jnp.where(j < lat_tiles, i, ctx_group), 0, 0, 0)
    kern = functools.partial(_proj_kernel, kv_lora=kv_lora, q_lora=q_lora, bw=bw, n_tok=len(toks),
                             lat_tiles=lat_tiles)
    names = ["norm_g", "w1", "kv_norm", "wk", "wv", "q_norm", "wq", "wqp", "qgain", "qgainp", "kgain", "kgainp"]
    return pl.pallas_call(
        kern,
        out_shape=[jax.ShapeDtypeStruct((b, HEADS, t, HEAD_PAD), BF16),
                   jax.ShapeDtypeStruct((b, HEADS, t, HEAD_PAD), BF16),
                   jax.ShapeDtypeStruct((b, HEADS, VAUG, t), BF16),
                   jax.ShapeDtypeStruct((t, b, bw), F32),
                   jax.ShapeDtypeStruct((b, t, 2 * bw), F32)],
        grid=(b // nb, t // tt),
        in_specs=_token_specs(toks, nb, tt, lat_tiles)
                 + [pl.BlockSpec((None, None, nb, 3, d), mod_map)]
                 + [_wspec(w[n], (li,)) for n in names]
                 + [pl.BlockSpec((tt, HEAD_PAD), lambda i, j: (j, 0)),
                    pl.BlockSpec((tt, HEAD_PAD), lambda i, j: (j, 0))],
        out_specs=[pl.BlockSpec((nb, HEADS, tt, HEAD_PAD), lambda i, j: (i, 0, j, 0)),
                   pl.BlockSpec((nb, HEADS, tt, HEAD_PAD), lambda i, j: (i, 0, j, 0)),
                   pl.BlockSpec((nb, HEADS, VAUG, tt), lambda i, j: (i, 0, 0, j)),
                   pl.BlockSpec((tt, nb, bw), lambda i, j: (j, i, 0)),
                   pl.BlockSpec((nb, tt, 2 * bw), lambda i, j: (i, j, 0))],
        compiler_params=_params(("arbitrary", "arbitrary")),
        name="proj_qkv",
    )(*toks, mod3, *[w[n] for n in names], cos_t, sin_t)


def _seq_kernel(xp_ref, cw_ref, cb_ref, pw_ref, pb_ref, ps_ref, xl_out, op_out,
                pad_ref, f2_ref, f4_ref, f8_ref, inv_ref, *, segs, bw):
    assert POOL_WINDOWS == (2, 4, 8, 16) and bw == 2 * LANE and HALO == 8
    gw = bw // len(POOL_WINDOWS)
    cw = cw_ref[...]
    left = LRU_CONV // 2

    @pl.when(pl.program_id(0) == 0)
    def _():
        for lo, t in segs:
            lane = lax.broadcasted_iota(jnp.int32, (t, bw), 1)
            row = lax.broadcasted_iota(jnp.int32, (t, bw), 0)
            half = jnp.ones((t, bw), jnp.int32)
            for gi in range(1, len(POOL_WINDOWS)):
                half = jnp.where(lane >= gi * gw, POOL_WINDOWS[gi] // 2, half)
            cnt = (jnp.minimum(row + half, t) - jnp.maximum(row - half, 0)).astype(F32)
            inv_ref[lo:lo + t, :] = 1.0 / cnt

    lane_half = lax.broadcasted_iota(jnp.int32, (1, LANE), 1) < gw
    for lo, t in segs:
        n = t + 2 * HALO
        pad_ref[0:HALO, :] = jnp.zeros((HALO, 2 * bw), F32)
        pad_ref[HALO + t:n + HALO, :] = jnp.zeros((2 * HALO, 2 * bw), F32)
        pad_ref[HALO:HALO + t, :] = xp_ref[0, lo:lo + t, :]
        f2_ref[n:n + HALO, :] = jnp.zeros((HALO, bw), F32)
        f4_ref[n:n + HALO, :] = jnp.zeros((HALO, bw), F32)
        f8_ref[n:n + HALO, :] = jnp.zeros((HALO, LANE), F32)

        acc = cb_ref[...] + pad_ref[HALO - left:HALO - left + t, 0:bw] * cw[0:1]
        for k in range(1, LRU_CONV):
            acc = acc + pad_ref[HALO + k - left:HALO + k - left + t, 0:bw] * cw[k:k + 1]
        xl_out[0, lo:lo + t, :] = acc

        f2_ref[0:n, :] = pad_ref[0:n, bw:2 * bw] + pad_ref[1:n + 1, bw:2 * bw]
        f4_ref[0:n, :] = f2_ref[0:n, :] + f2_ref[2:n + 2, :]
        f8_ref[0:n, :] = f4_ref[0:n, LANE:bw] + f4_ref[4:n + 4, LANE:bw]
        sel_lo = jnp.where(lane_half, f2_ref[7:7 + t, 0:LANE], f4_ref[6:6 + t, 0:LANE])
        sel_hi = jnp.where(lane_half, f8_ref[4:4 + t, :], f8_ref[0:t, :] + f8_ref[HALO:HALO + t, :])
        sel = jnp.concatenate([sel_lo, sel_hi], axis=1)
        p = sel * inv_ref[lo:lo + t, :] - pad_ref[HALO:HALO + t, bw:2 * bw]
        y = _dot(p.astype(BF16), pw_ref[...]) + pb_ref[...]
        op_out[0, lo:lo + t, :] = y * ps_ref[...]


def _seq(xp, w, li, segs):
    b, t, w2 = xp.shape
    bw = w2 // 2
    kern = functools.partial(_seq_kernel, segs=segs, bw=bw)
    names = ["conv_w", "conv_b", "pool_w", "pool_b", "pool_scale"]
    max_t = max(n for _, n in segs)
    return pl.pallas_call(
        kern,
        out_shape=[jax.ShapeDtypeStruct((b, t, bw), F32), jax.ShapeDtypeStruct((b, t, bw), F32)],
        grid=(b,),
        in_specs=[pl.BlockSpec((1, t, w2), lambda i: (i, 0, 0))] + [_wspec(w[n], (li,)) for n in names],
        out_specs=[pl.BlockSpec((1, t, bw), lambda i: (i, 0, 0)),
                   pl.BlockSpec((1, t, bw), lambda i: (i, 0, 0))],
        scratch_shapes=[pltpu.VMEM((max_t + 3 * HALO, w2), F32), pltpu.VMEM((max_t + 3 * HALO, bw), F32),
                        pltpu.VMEM((max_t + 3 * HALO, bw), F32), pltpu.VMEM((max_t + 3 * HALO, LANE), F32),
                        pltpu.VMEM((t, bw), F32)],
        compiler_params=_params(("arbitrary",)),
        name="conv_pool",
    )(xp, *[w[n] for n in names])


def _attn_kernel(q_ref, k_ref, vt_ref, o_ref):
    n_k = k_ref.shape[2]
    ck = min(KEY_CHUNK, n_k)
    nch = n_k // ck
    items = [(hh, c) for hh in range(HEADS) for c in range(nch)]

    def scores(item):
        hh, c = item
        return _dot_nt(k_ref[0, hh, c * ck:(c + 1) * ck, :], q_ref[0, hh])

    outs = []
    m = acc = None
    ahead = [scores(it) for it in items[:ATTN_LOOKAHEAD]]
    for i, (hh, c) in enumerate(items):
        sc = ahead.pop(0)
        if i + ATTN_LOOKAHEAD < len(items):
            ahead.append(scores(items[i + ATTN_LOOKAHEAD]))
        ks = slice(c * ck, (c + 1) * ck)
        mc = jnp.max(sc, axis=0, keepdims=True)
        if c == 0:
            m_new = mc
            acc = _dot(vt_ref[0, hh, :, ks], jnp.exp2(sc - m_new).astype(BF16))
        else:
            m_new = jnp.maximum(m, mc)
            acc = jnp.exp2(m - m_new) * acc + _dot(vt_ref[0, hh, :, ks], jnp.exp2(sc - m_new).astype(BF16))
        m = m_new
        if c == nch - 1:
            outs.append(acc[0:VDIM] / acc[VDIM:VDIM + 1])
    o_ref[0] = jnp.concatenate(outs, axis=0).T


def _attend(q, k, vt, n_q, q_off, n_k, k_off, tq):
    b = q.shape[0]
    hv = HEADS * VDIM
    assert n_q % tq == 0 and q_off % tq == 0 and k_off % n_k == 0
    qo, ko = q_off // tq, k_off // n_k
    return pl.pallas_call(
        _attn_kernel,
        out_shape=jax.ShapeDtypeStruct((b, n_q, hv), F32),
        grid=(b, n_q // tq),
        in_specs=[pl.BlockSpec((1, HEADS, tq, HEAD_PAD), lambda i, j: (i, 0, j + qo, 0)),
                  pl.BlockSpec((1, HEADS, n_k, HEAD_PAD), lambda i, j: (i, 0, ko, 0)),
                  pl.BlockSpec((1, HEADS, VAUG, n_k), lambda i, j: (i, 0, 0, ko))],
        out_specs=pl.BlockSpec((1, tq, hv), lambda i, j: (i, j, 0)),
        compiler_params=_params(("arbitrary", "arbitrary")),
        name="attention",
    )(q, k, vt)


def _scan_kernel(uf_ref, ub_ref, xf_ref, xb_ref, bmat_ref, cre_ref, cim_ref, ar_ref, ai_ref,
                 wa_ref, ba_ref, wx_ref, bx_ref, lam_ref, yf_ref, yb_ref,
                 hbuf, abuf, bbuf, hr_s, hi_s, hl_s, *, tc, bsz, bw, ns):
    r = tc * bsz
    sb = SCAN_SUB
    nsub = tc // sb
    rs = sb * bsz
    nblk = bw // LANE
    sblk = ns // nblk
    u_refs, x_refs, y_refs = (uf_ref, ub_ref), (xf_ref, xb_ref), (yf_ref, yb_ref)

    @pl.when(pl.program_id(0) == 0)
    def _():
        hr_s[...] = jnp.zeros_like(hr_s)
        hi_s[...] = jnp.zeros_like(hi_s)
        hl_s[...] = jnp.zeros_like(hl_s)

    u16 = [u_refs[d][...].reshape(r, bw).astype(BF16) for d in range(2)]
    xls = [jnp.transpose(x_refs[d][...], (1, 0, 2)).reshape(r, bw) for d in range(2)]
    softplus = []
    for d in range(2):
        nl = -lam_ref[d]
        softplus.append(jnp.maximum(nl, 0.0) + jnp.log1p(jnp.exp(-jnp.abs(nl))))

    def sub_rows(d, k):
        kk = (nsub - 1 - k) if d else k
        return kk, slice(kk * rs, (kk + 1) * rs)

    def input_stage(k):
        for d in range(2):
            _, rw = sub_rows(d, k)
            hbuf[d, rw, :] = _dot(u16[d][rw], bmat_ref[d])

    def gate_stage(k):
        for d in range(2):
            _, rw = sub_rows(d, k)
            xl = xls[d][rw]
            xb = xl.astype(BF16)
            rg = jax.nn.sigmoid(_dot(xb, wa_ref[d]) + ba_ref[d])
            ig = jax.nn.sigmoid(_dot(xb, wx_ref[d]) + bx_ref[d])
            log_a = (-LRU_C) * rg * softplus[d]
            abuf[d, rw, :] = jnp.exp(log_a)
            th = jnp.tanh(log_a)
            gain2 = -2.0 * th / (1.0 - th)
            gain = jnp.where(gain2 > 0.0, gain2 * lax.rsqrt(gain2), 0.0)
            bbuf[d, rw, :] = gain * (ig * xl)

    w = ns // SCAN_LANE_SPLITS
    lanes = [(slice(j * w, (j + 1) * w), slice(ns + j * w, ns + (j + 1) * w)) for j in range(SCAN_LANE_SPLITS)]
    coef = [[(jnp.broadcast_to(ar_ref[d, :, re_sl], (bsz, w)), jnp.broadcast_to(ai_ref[d, :, re_sl], (bsz, w)))
             for re_sl, _ in lanes] for d in range(2)]
    hl = [hl_s[d] for d in range(2)]

    def scan_stage(k):
        def step_rows(t, d):
            te = (tc - 1 - t) if d else t
            return te, slice(te * bsz, (te + 1) * bsz)

        for t in range(k * sb, (k + 1) * sb):
            for d in range(2):
                te, rw = step_rows(t, d)
                hl[d] = abuf[d, rw, :] * hl[d] + bbuf[d, rw, :]
                y_refs[d][te, :, bw:2 * bw] = hl[d]
        for j, (re_sl, im_sl) in enumerate(lanes):
            for t in range(k * sb, (k + 1) * sb):
                for d in range(2):
                    _, rw = step_rows(t, d)
                    if t == 0:
                        phr, phi = hr_s[d, :, re_sl], hi_s[d, :, re_sl]
                    else:
                        _, prw = step_rows(t - 1, d)
                        phr, phi = hbuf[d, prw, re_sl], hbuf[d, prw, im_sl]
                    ar, ai = coef[d][j]
                    hbuf[d, rw, re_sl] = ar * phr - ai * phi + hbuf[d, rw, re_sl]
                    hbuf[d, rw, im_sl] = ar * phi + ai * phr + hbuf[d, rw, im_sl]

    def readout_stage(k):
        for d in range(2):
            kk, rw = sub_rows(d, k)
            for jb in range(nblk):
                st = slice(jb * sblk, (jb + 1) * sblk)
                sti = slice(ns + jb * sblk, ns + (jb + 1) * sblk)
                ln = slice(jb * LANE, (jb + 1) * LANE)
                y = (_dot(hbuf[d, rw, st].astype(BF16), cre_ref[d, st, ln])
                     - _dot(hbuf[d, rw, sti].astype(BF16), cim_ref[d, st, ln]))
                y_refs[d][kk * sb:(kk + 1) * sb, :, ln] = y.reshape(sb, bsz, LANE)

    lead = min(2, nsub)
    for k in range(lead):
        input_stage(k)
    for k in range(lead):
        gate_stage(k)
    for k in range(nsub):
        if k + lead < nsub:
            input_stage(k + lead)
        scan_stage(k)
        if k >= 1:
            readout_stage(k - 1)
        if k + lead < nsub:
            gate_stage(k + lead)
    readout_stage(nsub - 1)

    for d in range(2):
        hl_s[d] = hl[d]
        last = slice((0 if d else tc - 1) * bsz, (1 if d else tc) * bsz)
        hr_s[d] = hbuf[d, last, 0:ns]
        hi_s[d] = hbuf[d, last, ns:2 * ns]


def _scan(u_tm, xl, w, li, n_lat):
    t, bsz, bw = u_tm.shape
    tc = SCAN_CHUNK
    nc = t // tc
    ncl = n_lat // tc
    ncc = nc - ncl
    ns = w["ar"].shape[-1]
    fwd = lambda s: jnp.where(s < ncc, ncl + s, s - ncc)
    bwd = lambda s: nc - 1 - s
    names = ["bmat", "cre", "cim", "ar", "ai", "wa", "ba", "wx", "bx", "lam"]
    r = tc * bsz
    kern = functools.partial(_scan_kernel, tc=tc, bsz=bsz, bw=bw, ns=ns)
    return pl.pallas_call(
        kern,
        out_shape=[jax.ShapeDtypeStruct((t, bsz, 2 * bw), F32)] * 2,
        grid=(nc,),
        in_specs=[pl.BlockSpec((tc, bsz, bw), lambda s: (fwd(s), 0, 0)),
                  pl.BlockSpec((tc, bsz, bw), lambda s: (bwd(s), 0, 0)),
                  pl.BlockSpec((bsz, tc, bw), lambda s: (0, fwd(s), 0)),
                  pl.BlockSpec((bsz, tc, bw), lambda s: (0, bwd(s), 0))]
                 + [_wspec(w[n], (li,)) for n in names],
        out_specs=[pl.BlockSpec((tc, bsz, 2 * bw), lambda s: (fwd(s), 0, 0)),
                   pl.BlockSpec((tc, bsz, 2 * bw), lambda s: (bwd(s), 0, 0))],
        scratch_shapes=[pltpu.VMEM((2, r, 2 * ns), F32), pltpu.VMEM((2, r, bw), F32), pltpu.VMEM((2, r, bw), F32),
                        pltpu.VMEM((2, bsz, ns), F32), pltpu.VMEM((2, bsz, ns), F32),
                        pltpu.VMEM((2, bsz, bw), F32)],
        compiler_params=_params(("arbitrary",)),
        name="scan_bidir",
    )(u_tm, u_tm, xl, xl, *[w[n] for n in names])


def _merge_kernel(*refs, bw, lat_tiles, with_ctx, n_tok):
    tok_refs, refs = refs[:n_tok], refs[n_tok:]
    if with_ctx:
        (mod_ref, ng_ref, om_ref, omc_ref, yf_ref, yb_ref, u_ref, op_ref,
         w2_ref, wb_ref, wo_ref, wglu_ref, d_ref, out_ref) = refs
    else:
        (mod_ref, ng_ref, om_ref, yf_ref, yb_ref, u_ref, op_ref,
         w2_ref, wb_ref, wo_ref, wglu_ref, d_ref, out_ref) = refs
    nb_all, tt, d = tok_refs[0].shape
    x_all = _read_tokens(tok_refs, lat_tiles)
    mod_all = mod_ref[...]
    rr_all = jnp.transpose(yf_ref[...] + yb_ref[...], (1, 0, 2))
    u_all = jnp.transpose(u_ref[...], (1, 0, 2))
    o_mla_all = om_ref[...]
    if with_ctx:
        o_mla_all = jnp.where(pl.program_id(1) >= lat_tiles, omc_ref[...], o_mla_all)
    op_all = op_ref[...]

    nb = nb_all // MERGE_ROW_SPLITS
    rows = nb * tt
    xs, hs, gates, branches = [], [], [], []
    for s in range(MERGE_ROW_SPLITS):
        bs = slice(s * nb, (s + 1) * nb)
        x = x_all[bs]
        mod = mod_all[bs]
        shift, scale, gate = mod[:, 0:1, :], mod[:, 1:2, :], mod[:, 2:3, :]
        h = (_rms(x, ng_ref[...]) * (1.0 + scale) + shift).reshape(rows, d).astype(BF16)
        rr = rr_all[bs].reshape(rows, 2 * bw)
        u = u_all[bs].reshape(rows, bw)
        g = jax.nn.gelu(rr[:, 0:bw] + d_ref[...] * u, approximate=True)
        o_s5 = g * jax.nn.sigmoid(_dot(g.astype(BF16), wglu_ref[...]))
        xs.append(x)
        hs.append(h)
        gates.append(gate)
        branches.append((o_mla_all[bs].reshape(rows, bw), o_s5, rr[:, bw:2 * bw], op_all[bs].reshape(rows, bw)))

    ys = [None] * MERGE_ROW_SPLITS
    for n in range(N_BRANCH):
        for s in range(MERGE_ROW_SPLITS):
            gp = _dot(hs[s], w2_ref[:, n * bw:(n + 1) * bw])
            off = N_BRANCH * bw + n * d
            ml = _dot(hs[s], w2_ref[:, off:off + d])
            tn = _dot((branches[s][n] * (gp * jax.nn.sigmoid(gp))).astype(BF16), wb_ref[n])
            term = jax.nn.sigmoid(ml) * tn
            ys[s] = term if ys[s] is None else ys[s] + term
    for s in range(MERGE_ROW_SPLITS):
        bs = slice(s * nb, (s + 1) * nb)
        out_ref[bs] = xs[s] + gates[s] * _dot(ys[s].astype(BF16), wo_ref[...]).reshape(nb, tt, d)


def _merge(toks, mod3, o_lat, o_ctx, yf, yb, u_tm, o_pool, w, li, n_lat, n_out):
    b, _, d = toks[0].shape
    nb, tt = SUBLANE, MERGE_STEPS
    bw = d // N_BRANCH
    with_ctx = o_ctx is not None
    lat_tiles = n_lat // tt
    ctx_group = mod3.shape[1] - 1
    mod_map = lambda i, j: (li, jnp.where(j < lat_tiles, i, ctx_group), 0, 0, 0)
    bm = lambda width: pl.BlockSpec((nb, tt, width), lambda i, j: (i, j, 0))
    tm = lambda width: pl.BlockSpec((tt, nb, width), lambda i, j: (j, i, 0))
    names = ["w2", "wb", "wo", "wglu", "s5_d"]
    in_specs = _token_specs(toks, nb, tt, lat_tiles) + [
        pl.BlockSpec((None, None, nb, 3, d), mod_map), _wspec(w["norm_g"], (li,)),
        pl.BlockSpec((nb, tt, bw), lambda i, j: (i, jnp.minimum(j, lat_tiles - 1), 0))]
    args = [*toks, mod3, w["norm_g"], o_lat]
    if with_ctx:
        ctx_tiles = o_ctx.shape[1] // tt
        in_specs.append(pl.BlockSpec((nb, tt, bw), lambda i, j: (i, jnp.clip(j - lat_tiles, 0, ctx_tiles - 1), 0)))
        args.append(o_ctx)
    in_specs += [tm(2 * bw), tm(2 * bw), tm(bw), bm(bw)] + [_wspec(w[n], (li,)) for n in names]
    args += [yf, yb, u_tm, o_pool] + [w[n] for n in names]
    kern = functools.partial(_merge_kernel, bw=bw, lat_tiles=lat_tiles, with_ctx=with_ctx, n_tok=len(toks))
    return pl.pallas_call(
        kern,
        out_shape=jax.ShapeDtypeStruct((b, n_out, d), F32),
        grid=(b // nb, n_out // tt),
        in_specs=in_specs,
        out_specs=bm(d),
        compiler_params=_params(("arbitrary", "arbitrary")),
        name="merge_out",
    )(*args)


def _block_diag(blocks):
    n, a, b = blocks.shape[-3:]
    eye = jnp.eye(n, dtype=blocks.dtype)
    return jnp.einsum("...nab,nm->...namb", blocks, eye).reshape(blocks.shape[:-3] + (n * a, n * b))


def _rope_perm():
    half = ROPE // 2
    nf = half // 2
    base = jnp.concatenate([jnp.arange(nf, half), jnp.arange(0, nf)])
    return jnp.concatenate([base, base + half])


def _head_pad(a, lo):
    n = a.shape[-1]
    pad = [(0, 0)] * (a.ndim - 1) + [(lo, HEAD_PAD - lo - n)]
    return jnp.pad(a, pad)


def _rope_tables(n_lat, n_ctx):
    rows_n = n_lat // GRID_W
    row = jnp.repeat(jnp.arange(rows_n, dtype=jnp.int32), GRID_W).astype(F32)
    col = jnp.tile(jnp.arange(GRID_W, dtype=jnp.int32), rows_n).astype(F32)
    nf = ROPE // 4
    inv = ROPE_THETA ** (-jnp.arange(nf, dtype=F32) / nf)
    ang_r = row[:, None] * inv
    ang_c = col[:, None] * inv
    cr, sr, cc, sc = jnp.cos(ang_r), jnp.sin(ang_r), jnp.cos(ang_c), jnp.sin(ang_c)
    ones = jnp.ones((n_lat, NOPE), F32)
    zeros = jnp.zeros((n_lat, NOPE), F32)
    tail = HEAD_PAD - QK
    cos = jnp.concatenate([ones, cr, cr, cc, cc, jnp.ones((n_lat, tail), F32)], -1)
    sin = jnp.concatenate([zeros, -sr, sr, -sc, sc, jnp.zeros((n_lat, tail), F32)], -1)
    cos = jnp.concatenate([cos, jnp.ones((n_ctx, HEAD_PAD), F32)], 0)
    sin = jnp.concatenate([sin, jnp.zeros((n_ctx, HEAD_PAD), F32)], 0)
    return cos, sin


def _prep_weights(p, disc):
    w_in = p["w_in"]
    depth, d, _ = w_in.shape
    bw = d // N_BRANCH
    kv_lora = p["mla_kv_norm"].shape[-1]
    q_lora = p["mla_q_norm"].shape[-1]
    perm = _rope_perm()
    off_krope = kv_lora
    off_s5 = off_krope + ROPE
    off_lru = off_s5 + bw
    off_cq = off_lru + bw
    off_pool = off_cq + q_lora
    off_gate = off_pool + bw
    w_mix = w_in[:, :, :off_gate].astype(BF16)
    w_krope = w_mix[:, :, off_krope:off_s5]
    w1 = jnp.concatenate([
        w_mix[:, :, :kv_lora],
        _head_pad(w_krope, NOPE),
        _head_pad(w_krope[:, :, perm], NOPE),
        w_mix[:, :, off_cq:off_pool],
        w_mix[:, :, off_s5:off_lru],
        w_mix[:, :, off_lru:off_cq],
        w_mix[:, :, off_pool:off_gate]], axis=2)
    w2 = w_in[:, :, off_gate:].astype(BF16)

    w_uq = p["mla_w_uq"].reshape(depth, q_lora, HEADS, QK)
    wq = _head_pad(w_uq, 0).reshape(depth, q_lora, HEADS * HEAD_PAD).astype(BF16)
    wqp = _head_pad(w_uq[..., NOPE:][..., perm], NOPE).reshape(depth, q_lora, HEADS * HEAD_PAD).astype(BF16)
    w_ukv = p["mla_w_ukv"].reshape(depth, kv_lora, HEADS, NOPE + VDIM)
    wk = _head_pad(w_ukv[..., :NOPE], 0).reshape(depth, kv_lora, HEADS * HEAD_PAD).astype(BF16)
    wv = w_ukv[..., NOPE:].reshape(depth, kv_lora, HEADS * VDIM).astype(BF16)
    q_gain, k_gain = p["mla_q_gain"], p["mla_k_gain"]
    row = lambda a: a[..., None, :]

    ab_re, ab_im, bb_re, bb_im = disc
    sw = lambda a: jnp.swapaxes(a, -1, -2)
    bmat = jnp.concatenate([_block_diag(sw(bb_re)), _block_diag(sw(bb_im))], axis=-1).astype(BF16)
    return dict(
        norm_g=row(p["norm_g"]), w1=w1, w2=w2,
        kv_norm=row(p["mla_kv_norm"]), q_norm=row(p["mla_q_norm"]),
        wq=wq, wqp=wqp, wk=wk, wv=wv,
        qgain=row(_head_pad(q_gain, 0)), qgainp=row(_head_pad(q_gain[:, NOPE:][:, perm], NOPE)),
        kgain=row(_head_pad(k_gain, 0)), kgainp=row(_head_pad(k_gain[:, NOPE:][:, perm], NOPE)),
        conv_w=p["lru_conv_w"], conv_b=row(p["lru_conv_b"]),
        pool_w=_block_diag(p["pool_w"]).astype(BF16), pool_b=row(p["pool_b"]),
        pool_scale=row(p["pool_scale"]),
        wb=p["w_branch"].astype(BF16), wo=p["w_out"].astype(BF16),
        wglu=p["s5_w_glu"].astype(BF16), s5_d=row(p["s5_d"]),
        bmat=bmat,
        cre=_block_diag(sw(p["s5_c_re"])).astype(BF16), cim=_block_diag(sw(p["s5_c_im"])).astype(BF16),
        ar=row(ab_re.reshape(ab_re.shape[:2] + (-1,))), ai=row(ab_im.reshape(ab_im.shape[:2] + (-1,))),
        wa=_block_diag(p["lru_w_a"]).astype(BF16), ba=row(p["lru_b_a"]),
        wx=_block_diag(p["lru_w_x"]).astype(BF16), bx=row(p["lru_b_x"]),
        lam=row(p["lru_lambda"]))


def kernel(x, c, ctx, c_ctx, w_ada, b_ada, norm_g, w_in, mla_q_norm, mla_kv_norm, mla_w_uq, mla_w_ukv, mla_q_gain, mla_k_gain, s5_a_re, s5_a_im, s5_log_dt, s5_b_re, s5_b_im, s5_c_re, s5_c_im, s5_d, s5_w_glu, lru_conv_w, lru_conv_b, lru_lambda, lru_w_a, lru_b_a, lru_w_x, lru_b_x, pool_w, pool_b, pool_scale, w_branch, w_out):
    p = dict(w_in=w_in, norm_g=norm_g, mla_q_norm=mla_q_norm, mla_kv_norm=mla_kv_norm,
             mla_w_uq=mla_w_uq, mla_w_ukv=mla_w_ukv, mla_q_gain=mla_q_gain, mla_k_gain=mla_k_gain,
             s5_c_re=s5_c_re, s5_c_im=s5_c_im, s5_d=s5_d, s5_w_glu=s5_w_glu,
             lru_conv_w=lru_conv_w, lru_conv_b=lru_conv_b, lru_lambda=lru_lambda,
             lru_w_a=lru_w_a, lru_b_a=lru_b_a, lru_w_x=lru_w_x, lru_b_x=lru_b_x,
             pool_w=pool_w, pool_b=pool_b, pool_scale=pool_scale, w_branch=w_branch, w_out=w_out)
    depth = w_in.shape[0]
    b, l, d = x.shape
    lc = ctx.shape[1]
    t = l + lc
    assert b % SUBLANE == 0 and l % PROJ_STEPS == 0 and lc % PROJ_STEPS == 0 and l % lc == 0

    n_rows = -(-(b + 1) // SUBLANE) * SUBLANE
    cc = jnp.concatenate([c, c_ctx[None], jnp.zeros((n_rows - b - 1, d), F32)], axis=0)
    mod_all = _ada(cc, w_ada, b_ada)
    mod3 = jnp.concatenate([
        mod_all[:, :b].reshape(depth, b // SUBLANE, SUBLANE, 3, d),
        jnp.broadcast_to(mod_all[:, b:b + 1].reshape(depth, 1, 1, 3, d), (depth, 1, SUBLANE, 3, d))], axis=1)
    disc = _s5_discretize(s5_a_re, s5_a_im, s5_log_dt, s5_b_re, s5_b_im)
    w = _prep_weights(p, disc)
    cos_t, sin_t = _rope_tables(l, lc)
    tq_l, tq_c = min(Q_TILE, l), min(Q_TILE, lc)

    toks = (x, ctx)
    for li in range(depth):
        with_ctx = li < depth - 1
        q, k, vt, u_tm, xp = _proj(toks, mod3, w, li, cos_t, sin_t, l)
        xl, o_pool = _seq(xp, w, li, ((0, l), (l, lc)))
        o_lat = _attend(q, k, vt, l, 0, t, 0, tq_l)
        o_ctx = _attend(q, k, vt, lc, l, lc, l, tq_c) if with_ctx else None
        yf, yb = _scan(u_tm, xl, w, li, l)
        toks = (_merge(toks, mod3, o_lat, o_ctx, yf, yb, u_tm, o_pool, w, li, l, t if with_ctx else l),)
    return toks[0]
```
